```python
import jax, jax.numpy as jnp
from jax import lax
import numpy as np

D_MODEL = 1024
BATCH = 8
SEQ = 4096
DEPTH = 4

N_MIXERS = 3
GRID_W = 64
D_FF = 2816
NORM_EPS = 1e-6
POOL_WINDOWS = (2, 4, 8, 16)
N_POOL_GROUPS = 4
POOL_GROUP = D_MODEL // N_POOL_GROUPS
N_FOURIER_GROUPS = 4
FOURIER_GROUP = D_MODEL // N_FOURIER_GROUPS
HEAD_DIM = 128
N_Q_HEADS = D_MODEL // HEAD_DIM
N_KV_HEADS = N_Q_HEADS // 4
Q_PER_KV = N_Q_HEADS // N_KV_HEADS
D_Q = N_Q_HEADS * HEAD_DIM
D_KV = N_KV_HEADS * HEAD_DIM
Q_BLOCK = 128
ROPE_THETA = 10000.0

kernel_name = "hybrid_pool_fourier_gqa_macaron_encoder"


def rms_norm(x, gain):
    xf = x.astype(jnp.float32)
    y = xf * lax.rsqrt(jnp.mean(xf * xf, axis=-1, keepdims=True) + NORM_EPS)
    return (y * gain.astype(jnp.float32)).astype(x.dtype)


def swiglu(h, w_gate, w_up, w_down):
    return (jax.nn.silu(h @ w_gate) * (h @ w_up)) @ w_down


def pool_mixer(h, w_grp, b_grp, scale):
    B, S, D = h.shape
    hf = h.astype(jnp.float32)
    csum = jnp.concatenate([jnp.zeros((B, 1, D), jnp.float32), jnp.cumsum(hf, axis=1)], axis=1)
    t = jnp.arange(S)
    outs = []
    for g, w in enumerate(POOL_WINDOWS):
        lo = jnp.clip(t - w // 2, 0, S)
        hi = jnp.clip(t + w // 2, 0, S)
        cg = csum[..., g * POOL_GROUP:(g + 1) * POOL_GROUP]
        seg = jnp.take(cg, hi, axis=1) - jnp.take(cg, lo, axis=1)
        cnt = (hi - lo).astype(jnp.float32)
        outs.append(seg / cnt[None, :, None])
    pooled = (jnp.concatenate(outs, axis=-1) - hf).astype(h.dtype)
    pooled = pooled.reshape(B, S, N_POOL_GROUPS, POOL_GROUP)
    y = jnp.einsum('bsgc,gcd->bsgd', pooled, w_grp) + b_grp
    return y.reshape(B, S, D) * scale


def fourier_mixer(h, w_out, b_out):
    B, S, D = h.shape
    hf = h.astype(jnp.float32).reshape(B, S, N_FOURIER_GROUPS, FOURIER_GROUP)
    f = jnp.fft.fft2(hf, axes=(1, 3), norm='ortho').real
    f = f.astype(h.dtype).reshape(B, S, D)
    return f @ w_out + b_out


def axial_rope_tables(seq_len, dtype):
    rows = seq_len // GRID_W
    row = jnp.repeat(jnp.arange(rows, dtype=jnp.float32), GRID_W)
    col = jnp.tile(jnp.arange(GRID_W, dtype=jnp.float32), rows)
    half = HEAD_DIM // 2
    inv_freq = ROPE_THETA ** (-jnp.arange(0, half, 2, dtype=jnp.float32) / half)
    ang_r = row[:, None] * inv_freq[None, :]
    ang_c = col[:, None] * inv_freq[None, :]
    ang = jnp.concatenate([ang_r, ang_r, ang_c, ang_c], axis=-1)
    return jnp.cos(ang).astype(dtype), jnp.sin(ang).astype(dtype)


def _rotate_half(u):
    u1, u2 = jnp.split(u, 2, axis=-1)
    return jnp.concatenate([-u2, u1], axis=-1)


def apply_axial_rope(x, cos, sin):
    half = HEAD_DIM // 2
    rot = jnp.concatenate([_rotate_half(x[..., :half]), _rotate_half(x[..., half:])], axis=-1)
    return x * cos[:, None, :] + rot * sin[:, None, :]


def gqa_axial_attention(h, w_qkv, q_gain, k_gain, w_o, cos, sin):
    B, S, D = h.shape
    n_blk = S // Q_BLOCK
    qkv = h @ w_qkv
    q = qkv[..., :D_Q].reshape(B, S, N_Q_HEADS, HEAD_DIM)
    k = qkv[..., D_Q:D_Q + D_KV].reshape(B, S, N_KV_HEADS, HEAD_DIM)
    v = qkv[..., D_Q + D_KV:].reshape(B, S, N_KV_HEADS, HEAD_DIM)
    q = apply_axial_rope(rms_norm(q, q_gain), cos, sin) * (HEAD_DIM ** -0.5)
    k = apply_axial_rope(rms_norm(k, k_gain), cos, sin)
    q = q.reshape(B, n_blk, Q_BLOCK, N_KV_HEADS, Q_PER_KV, HEAD_DIM).transpose(1, 0, 2, 3, 4, 5)

    def attend(qb):
        s = jnp.einsum('bqgrd,bkgd->bgrqk', qb, k).astype(jnp.float32)
        p = jax.nn.softmax(s, axis=-1).astype(v.dtype)
        return jnp.einsum('bgrqk,bkgd->bqgrd', p, v)

    o = lax.map(attend, q)
    o = o.transpose(1, 0, 2, 3, 4, 5).reshape(B, S, D_Q)
    return o @ w_o


def setup_inputs(seed: int = 0) -> dict:
    key = jax.random.key(seed)
    ks = jax.random.split(key, 24)
    n_pool = len(range(0, DEPTH, N_MIXERS))
    n_fourier = len(range(1, DEPTH, N_MIXERS))
    n_attn = len(range(2, DEPTH, N_MIXERS))

    def nrm(k, shape, fan_in):
        return jax.random.normal(k, shape, jnp.float32) * (fan_in ** -0.5)

    def gain(k, shape):
        return 1.0 + 0.02 * jax.random.normal(k, shape, jnp.float32)

    def small(k, shape):
        return 0.01 * jax.random.normal(k, shape, jnp.float32)

    return {
        'x': jax.random.normal(ks[0], (BATCH, SEQ, D_MODEL), jnp.float32),
        'ffn1_norm': gain(ks[1], (DEPTH, D_MODEL)),
        'ffn1_w_gate': nrm(ks[2], (DEPTH, D_MODEL, D_FF), D_MODEL),
        'ffn1_w_up': nrm(ks[3], (DEPTH, D_MODEL, D_FF), D_MODEL),
        'ffn1_w_down': nrm(ks[4], (DEPTH, D_FF, D_MODEL), D_FF),
        'mixer_norm': gain(ks[5], (DEPTH, D_MODEL)),
        'ffn2_norm': gain(ks[6], (DEPTH, D_MODEL)),
        'ffn2_w_gate': nrm(ks[7], (DEPTH, D_MODEL, D_FF), D_MODEL),
        'ffn2_w_up': nrm(ks[8], (DEPTH, D_MODEL, D_FF), D_MODEL),
        'ffn2_w_down': nrm(ks[9], (DEPTH, D_FF, D_MODEL), D_FF),
        'pool_w': nrm(ks[10], (n_pool, N_POOL_GROUPS, POOL_GROUP, POOL_GROUP), POOL_GROUP),
        'pool_b': small(ks[11], (n_pool, N_POOL_GROUPS, POOL_GROUP)),
        'pool_scale': gain(ks[12], (n_pool, D_MODEL)),
        'fourier_w': nrm(ks[13], (n_fourier, D_MODEL, D_MODEL), D_MODEL),
        'fourier_b': small(ks[14], (n_fourier, D_MODEL)),
        'attn_w_qkv': nrm(ks[15], (n_attn, D_MODEL, D_Q + 2 * D_KV), D_MODEL),
        'attn_q_norm': gain(ks[16], (n_attn, HEAD_DIM)),
        'attn_k_norm': gain(ks[17], (n_attn, HEAD_DIM)),
        'attn_w_o': nrm(ks[18], (n_attn, D_Q, D_MODEL), D_Q),
        'final_norm': gain(ks[19], (D_MODEL,)),
    }


def reference(x, ffn1_norm, ffn1_w_gate, ffn1_w_up, ffn1_w_down, mixer_norm,
              ffn2_norm, ffn2_w_gate, ffn2_w_up, ffn2_w_down,
              pool_w, pool_b, pool_scale, fourier_w, fourier_b,
              attn_w_qkv, attn_q_norm, attn_k_norm, attn_w_o, final_norm):
    cos, sin = axial_rope_tables(x.shape[1], x.dtype)
    for i in range(DEPTH):
        h = rms_norm(x, ffn1_norm[i])
        x = x + 0.5 * swiglu(h, ffn1_w_gate[i], ffn1_w_up[i], ffn1_w_down[i])
        h = rms_norm(x, mixer_norm[i])
        kind = i % N_MIXERS
        j = i // N_MIXERS
        if kind == 0:
            y = pool_mixer(h, pool_w[j], pool_b[j], pool_scale[j])
        elif kind == 1:
            y = fourier_mixer(h, fourier_w[j], fourier_b[j])
        else:
            y = gqa_axial_attention(h, attn_w_qkv[j], attn_q_norm[j], attn_k_norm[j],
                                    attn_w_o[j], cos, sin)
        x = x + y
        h = rms_norm(x, ffn2_norm[i])
        x = x + 0.5 * swiglu(h, ffn2_w_gate[i], ffn2_w_up[i], ffn2_w_down[i])
    return rms_norm(x, final_norm)
```

```python
import functools
import math

import jax
import jax.numpy as jnp
from jax import lax
from jax.experimental import pallas as pl
from jax.experimental.pallas import tpu as pltpu

D_MODEL = 1024
D_FF = 2816
N_MIXERS = 3
NORM_EPS = 1e-6
POOL_WINDOWS = (2, 4, 8, 16)
POOL_GROUP = D_MODEL // len(POOL_WINDOWS)
POOL_HALO = max(POOL_WINDOWS) // 2
N_FOURIER_GROUPS = 4
FOURIER_GROUP = D_MODEL // N_FOURIER_GROUPS
HEAD_DIM = 128
N_Q_HEADS = D_MODEL // HEAD_DIM
N_KV_HEADS = N_Q_HEADS // 4
Q_PER_KV = N_Q_HEADS // N_KV_HEADS
D_Q = N_Q_HEADS * HEAD_DIM
D_KV = N_KV_HEADS * HEAD_DIM
GRID_W = 64
ROPE_THETA = 10000.0

V7X_MXU_DIM = 256
V7X_SUBLANES = 8
V7X_VMEM_BYTES = 64 * 1024 * 1024
VMEM_LIMIT_BYTES = V7X_VMEM_BYTES * 7 // 8

FFN_ROWS = 512
FFN_COLS = V7X_MXU_DIM
POOL_ROWS = 256
TOKEN_ROWS = 512
DFT_ROWS = 256
ATTN_Q_ROWS = 256

F32 = jnp.float32
BF16 = jnp.bfloat16


def _params(*semantics):
    return pltpu.CompilerParams(dimension_semantics=semantics,
                                vmem_limit_bytes=VMEM_LIMIT_BYTES)


def _resident(shape):
    zeros = (0,) * len(shape)
    return pl.BlockSpec(shape, lambda *_: zeros, pipeline_mode=pl.Buffered(1))


def _rms(x, gain):
    ms = jnp.mean(x * x, axis=-1, keepdims=True)
    return x * lax.rsqrt(ms + NORM_EPS) * gain


def _dot(a, b):
    return jnp.dot(a, b, preferred_element_type=F32)


def _ffn_body(*refs, final_norm):
    if final_norm:
        x_ref, g_ref, wg_ref, wu_ref, wd_ref, fg_ref, o_ref, act_ref = refs
    else:
        x_ref, g_ref, wg_ref, wu_ref, wd_ref, o_ref, act_ref = refs
    x = x_ref[...]
    h = _rms(x, g_ref[...]).astype(BF16)
    for c in range(D_FF // FFN_COLS):
        cols = slice(c * FFN_COLS, (c + 1) * FFN_COLS)
        gate = _dot(h, wg_ref[:, cols])
        up = _dot(h, wu_ref[:, cols])
        act_ref[:, cols] = (gate * jax.nn.sigmoid(gate) * up).astype(BF16)
    out = x + 0.5 * _dot(act_ref[...], wd_ref[...])
    if final_norm:
        out = _rms(out, fg_ref[...])
    o_ref[...] = out


def _ffn(x2d, gain, wg, wu, wd, final_gain=None):
    n = x2d.shape[0]
    row_spec = pl.BlockSpec((FFN_ROWS, D_MODEL), lambda i: (i, 0))
    in_specs = [row_spec, _resident((1, D_MODEL)), _resident((D_MODEL, D_FF)),
                _resident((D_MODEL, D_FF)), _resident((D_FF, D_MODEL))]
    args = [x2d, gain.reshape(1, D_MODEL), wg, wu, wd]
    if final_gain is not None:
        in_specs.append(_resident((1, D_MODEL)))
        args.append(final_gain.reshape(1, D_MODEL))
    return pl.pallas_call(
        functools.partial(_ffn_body, final_norm=final_gain is not None),
        out_shape=jax.ShapeDtypeStruct(x2d.shape, F32),
        grid=(n // FFN_ROWS,),
        in_specs=in_specs,
        out_specs=row_spec,
        scratch_shapes=[pltpu.VMEM((FFN_ROWS, D_FF), BF16)],
        compiler_params=_params("parallel"),
        name="ffn_final" if final_gain is not None else "ffn",
    )(*args)


def _pool_bands():
    t = jnp.arange(POOL_ROWS)[:, None]
    r = jnp.arange(POOL_ROWS + 2 * POOL_HALO)[None, :] - POOL_HALO
    bands = [((r >= t - w // 2) & (r < t + w // 2)) for w in POOL_WINDOWS]
    return jnp.stack(bands).astype(BF16)


def _pool_body(xm_ref, xp_ref, xn_ref, g_ref, band_ref, pw_ref, pb_ref, ps_ref,
               o_ref, ext_ref, *, seq):
    i = pl.program_id(1)
    gain = g_ref[...]
    xm = xm_ref[0]
    hm = _rms(xm, gain)
    hp = jnp.where(i > 0, _rms(xp_ref[0], gain), 0.0)
    hn = jnp.where(i < pl.num_programs(1) - 1, _rms(xn_ref[0], gain), 0.0)
    ext_ref[0:POOL_HALO] = hp
    ext_ref[POOL_HALO:POOL_HALO + POOL_ROWS] = hm
    ext_ref[POOL_HALO + POOL_ROWS:] = hn
    ext = ext_ref[...]
    ext_hi = ext.astype(BF16)
    ext_lo = (ext - ext_hi.astype(F32)).astype(BF16)
    t = i * POOL_ROWS + lax.broadcasted_iota(jnp.int32, (POOL_ROWS, 1), 0)
    outs = []
    for g, w in enumerate(POOL_WINDOWS):
        cols = slice(g * POOL_GROUP, (g + 1) * POOL_GROUP)
        band = band_ref[g]
        wsum = _dot(band, ext_hi[:, cols]) + _dot(band, ext_lo[:, cols])
        cnt = jnp.minimum(t + w // 2, seq) - jnp.maximum(t - w // 2, 0)
        pooled = wsum / cnt.astype(F32) - hm[:, cols]
        outs.append(_dot(pooled.astype(BF16), pw_ref[g]) + pb_ref[g])
    y = jnp.concatenate(outs, axis=-1) * ps_ref[...]
    o_ref[0] = xm + y


def _pool_mixer(x, gain, pw, pb, ps):
    b, s, _ = x.shape
    blocks_per_tile = POOL_ROWS // POOL_HALO
    n_halo_blocks = s // POOL_HALO
    n_groups = len(POOL_WINDOWS)
    ext_rows = POOL_ROWS + 2 * POOL_HALO
    return pl.pallas_call(
        functools.partial(_pool_body, seq=s),
        out_shape=jax.ShapeDtypeStruct(x.shape, F32),
        grid=(b, s // POOL_ROWS),
        in_specs=[
            pl.BlockSpec((1, POOL_ROWS, D_MODEL), lambda bi, i: (bi, i, 0)),
            pl.BlockSpec((1, POOL_HALO, D_MODEL),
                         lambda bi, i: (bi, jnp.maximum(i * blocks_per_tile - 1, 0), 0)),
            pl.BlockSpec((1, POOL_HALO, D_MODEL),
                         lambda bi, i: (bi, jnp.minimum((i + 1) * blocks_per_tile,
                                                        n_halo_blocks - 1), 0)),
            _resident((1, D_MODEL)),
            _resident((n_groups, POOL_ROWS, ext_rows)),
            _resident((n_groups, POOL_GROUP, POOL_GROUP)),
            _resident((n_groups, 1, POOL_GROUP)),
            _resident((1, D_MODEL)),
        ],
        out_specs=pl.BlockSpec((1, POOL_ROWS, D_MODEL), lambda bi, i: (bi, i, 0)),
        scratch_shapes=[pltpu.VMEM((ext_rows, D_MODEL), F32)],
        compiler_params=_params("parallel", "parallel"),
        name="pool_mixer",
    )(x, x, x, gain.reshape(1, D_MODEL), _pool_bands(), pw.astype(BF16),
      pb.reshape(n_groups, 1, POOL_GROUP), ps.reshape(1, D_MODEL))


def _dft_tables(n_rows, n):
    k = (jnp.arange(n_rows, dtype=jnp.int32)[:, None] * jnp.arange(n, dtype=jnp.int32)[None, :]) % n
    ang = k.astype(F32) * (2.0 * math.pi / n)
    return jnp.cos(ang), jnp.sin(ang)


def _seq_dft_matrix(s):
    r = int(math.isqrt(s))
    assert r * r == s
    ac, as_ = _dft_tables(r, r)
    ac = jnp.tile(ac, (1, s // r))
    as_ = jnp.tile(as_, (1, s // r))
    bc, bs = _dft_tables(r, s)
    cos = ac[:, None, :] * bc[None, :, :] - as_[:, None, :] * bs[None, :, :]
    sin = as_[:, None, :] * bc[None, :, :] + ac[:, None, :] * bs[None, :, :]
    return jnp.concatenate([cos.reshape(s, s), -sin.reshape(s, s)], axis=1).astype(BF16)


def _chan_dft_body(x_ref, g_ref, cc_ref, sc_ref, ab_ref):
    h = _rms(x_ref[0], g_ref[...]).astype(BF16)
    for g in range(N_FOURIER_GROUPS):
        cols = slice(g * FOURIER_GROUP, (g + 1) * FOURIER_GROUP)
        ab_ref[0, 0, :, cols] = _dot(h[:, cols], cc_ref[...]).astype(BF16)
        ab_ref[0, 1, :, cols] = _dot(h[:, cols], sc_ref[...]).astype(BF16)


def _seq_dft_body(w_ref, ab_ref, x_ref, wo_ref, bo_ref, o_ref, *, inv_norm):
    f = _dot(w_ref[...], ab_ref[0]) * inv_norm
    o_ref[0] = x_ref[0] + _dot(f.astype(BF16), wo_ref[...]) + bo_ref[...]


def _fourier_mixer(x, gain, w_out, b_out):
    b, s, _ = x.shape
    cc, sc = _dft_tables(FOURIER_GROUP, FOURIER_GROUP)
    ab = pl.pallas_call(
        _chan_dft_body,
        out_shape=jax.ShapeDtypeStruct((b, 2, s, D_MODEL), BF16),
        grid=(b, s // TOKEN_ROWS),
        in_specs=[pl.BlockSpec((1, TOKEN_ROWS, D_MODEL), lambda bi, i: (bi, i, 0)),
                  _resident((1, D_MODEL)),
                  _resident((FOURIER_GROUP, FOURIER_GROUP)),
                  _resident((FOURIER_GROUP, FOURIER_GROUP))],
        out_specs=pl.BlockSpec((1, 2, TOKEN_ROWS, D_MODEL), lambda bi, i: (bi, 0, i, 0)),
        compiler_params=_params("parallel", "parallel"),
        name="fourier_chan_dft",
    )(x, gain.reshape(1, D_MODEL), cc.astype(BF16), sc.astype(BF16))
    ab = ab.reshape(b, 2 * s, D_MODEL)
    inv_norm = 1.0 / math.sqrt(s * FOURIER_GROUP)
    return pl.pallas_call(
        functools.partial(_seq_dft_body, inv_norm=inv_norm),
        out_shape=jax.ShapeDtypeStruct(x.shape, F32),
        grid=(b, s // DFT_ROWS),
        in_specs=[pl.BlockSpec((DFT_ROWS, 2 * s), lambda bi, i: (i, 0)),
                  pl.BlockSpec((1, 2 * s, D_MODEL), lambda bi, i: (bi, 0, 0),
                               pipeline_mode=pl.Buffered(1)),
                  pl.BlockSpec((1, DFT_ROWS, D_MODEL), lambda bi, i: (bi, i, 0)),
                  _resident((D_MODEL, D_MODEL)),
                  _resident((1, D_MODEL))],
        out_specs=pl.BlockSpec((1, DFT_ROWS, D_MODEL), lambda bi, i: (bi, i, 0)),
        compiler_params=_params("parallel", "arbitrary"),
        name="fourier_seq_dft",
    )(_seq_dft_matrix(s), ab, x, w_out.astype(BF16), b_out.reshape(1, D_MODEL))


def _rope_tables(s):
    rows = s // GRID_W
    row = jnp.repeat(jnp.arange(rows, dtype=F32), GRID_W)
    col = jnp.tile(jnp.arange(GRID_W, dtype=F32), rows)
    half = HEAD_DIM // 2
    inv_freq = ROPE_THETA ** (-jnp.arange(0, half, 2, dtype=F32) / half)
    ang_r = row[:, None] * inv_freq[None, :]
    ang_c = col[:, None] * inv_freq[None, :]
    ang = jnp.concatenate([ang_r, ang_r, ang_c, ang_c], axis=-1)
    cos, sin = jnp.cos(ang), jnp.sin(ang)
    first = (jnp.arange(HEAD_DIM) % half) < half // 2
    sin_up = jnp.where(first[None, :], -sin, 0.0)
    sin_dn = jnp.where(first[None, :], 0.0, sin)
    return cos, sin_up, sin_dn


def _qkv_body(x_ref, g_ref, w_ref, qg_ref, kg_ref, cos_ref, sup_ref, sdn_ref,
              q_ref, k_ref, v_ref):
    h = _rms(x_ref[...], g_ref[...]).astype(BF16)
    qkv = _dot(h, w_ref[...])
    cos, sin_up, sin_dn = cos_ref[...], sup_ref[...], sdn_ref[...]
    quarter = HEAD_DIM // 4

    def norm_rope(u, gain):
        ms = jnp.mean(u * u, axis=-1, keepdims=True)
        u = u * lax.rsqrt(ms + NORM_EPS) * gain
        up = pltpu.roll(u, HEAD_DIM - quarter, 1)
        dn = pltpu.roll(u, quarter, 1)
        return u * cos + (up * sin_up + dn * sin_dn)

    scale = HEAD_DIM ** -0.5
    for hd in range(N_Q_HEADS):
        cols = slice(hd * HEAD_DIM, (hd + 1) * HEAD_DIM)
        q_ref[:, cols] = (norm_rope(qkv[:, cols], qg_ref[...]) * scale).astype(BF16)
    for hd in range(N_KV_HEADS):
        cols = slice(hd * HEAD_DIM, (hd + 1) * HEAD_DIM)
        src = slice(D_Q + hd * HEAD_DIM, D_Q + (hd + 1) * HEAD_DIM)
        k_ref[:, cols] = norm_rope(qkv[:, src], kg_ref[...]).astype(BF16)
    v_ref[...] = qkv[:, D_Q + D_KV:].astype(BF16)


def _attn_body(q_ref, k_ref, v_ref, o_ref):
    k = k_ref[0]
    v = v_ref[0]
    for r in range(Q_PER_KV):
        cols = slice(r * HEAD_DIM, (r + 1) * HEAD_DIM)
        s = lax.dot_general(q_ref[0, :, cols], k, (((1,), (1,)), ((), ())),
                            preferred_element_type=F32)
        m = jnp.max(s, axis=-1, keepdims=True)
        p = jnp.exp(s - m)
        l = jnp.sum(p, axis=-1, keepdims=True)
        o_ref[0, :, cols] = (_dot(p.astype(BF16), v) / l).astype(BF16)


def _proj_body(a_ref, x_ref, w_ref, o_ref):
    o_ref[...] = x_ref[...] + _dot(a_ref[...], w_ref[...])


def _attention_mixer(x, gain, w_qkv, q_gain, k_gain, w_o):
    b, s, _ = x.shape
    n = b * s
    x2d = x.reshape(n, D_MODEL)
    cos, sin_up, sin_dn = _rope_tables(s)
    tiles_per_seq = s // TOKEN_ROWS
    row = lambda width: pl.BlockSpec((TOKEN_ROWS, width), lambda i: (i, 0))
    pos = pl.BlockSpec((TOKEN_ROWS, HEAD_DIM), lambda i: (i % tiles_per_seq, 0))
    q, k, v = pl.pallas_call(
        _qkv_body,
        out_shape=(jax.ShapeDtypeStruct((n, D_Q), BF16),
                   jax.ShapeDtypeStruct((n, D_KV), BF16),
                   jax.ShapeDtypeStruct((n, D_KV), BF16)),
        grid=(n // TOKEN_ROWS,),
        in_specs=[row(D_MODEL), _resident((1, D_MODEL)),
                  _resident((D_MODEL, D_Q + 2 * D_KV)),
                  _resident((1, HEAD_DIM)), _resident((1, HEAD_DIM)), pos, pos, pos],
        out_specs=(row(D_Q), row(D_KV), row(D_KV)),
        compiler_params=_params("parallel"),
        name="attn_qkv",
    )(x2d, gain.reshape(1, D_MODEL), w_qkv.astype(BF16), q_gain.reshape(1, HEAD_DIM),
      k_gain.reshape(1, HEAD_DIM), cos, sin_up, sin_dn)

    group_w = Q_PER_KV * HEAD_DIM
    attn = pl.pallas_call(
        _attn_body,
        out_shape=jax.ShapeDtypeStruct((b, s, D_Q), BF16),
        grid=(b, N_KV_HEADS, s // ATTN_Q_ROWS),
        in_specs=[pl.BlockSpec((1, ATTN_Q_ROWS, group_w), lambda bi, g, i: (bi, i, g)),
                  pl.BlockSpec((1, s, HEAD_DIM), lambda bi, g, i: (bi, 0, g)),
                  pl.BlockSpec((1, s, HEAD_DIM), lambda bi, g, i: (bi, 0, g))],
        out_specs=pl.BlockSpec((1, ATTN_Q_ROWS, group_w), lambda bi, g, i: (bi, i, g)),
        compiler_params=_params("parallel", "parallel", "arbitrary"),
        name="attn_core",
    )(q.reshape(b, s, D_Q), k.reshape(b, s, D_KV), v.reshape(b, s, D_KV))

    out = pl.pallas_call(
        _proj_body,
        out_shape=jax.ShapeDtypeStruct((n, D_MODEL), F32),
        grid=(n // TOKEN_ROWS,),
        in_specs=[row(D_Q), row(D_MODEL), _resident((D_Q, D_MODEL))],
        out_specs=row(D_MODEL),
        compiler_params=_params("parallel"),
        name="attn_out_proj",
    )(attn.reshape(n, D_Q), x2d, w_o.astype(BF16))
    return out.reshape(b, s, D_MODEL)


def kernel(x, ffn1_norm, ffn1_w_gate, ffn1_w_up, ffn1_w_down, mixer_norm, ffn2_norm, ffn2_w_gate, ffn2_w_up, ffn2_w_down, pool_w, pool_b, pool_scale, fourier_w, fourier_b, attn_w_qkv, attn_q_norm, attn_k_norm, attn_w_o, final_norm):
    b, s, d = x.shape
    depth = ffn1_norm.shape[0]
    n = b * s

    def ffn(x3d, gain, wg, wu, wd, final_gain=None):
        out = _ffn(x3d.reshape(n, d), gain, wg.astype(BF16), wu.astype(BF16),
                   wd.astype(BF16), final_gain)
        return out.reshape(b, s, d)

    for i in range(depth):
        x = ffn(x, ffn1_norm[i], ffn1_w_gate[i], ffn1_w_up[i], ffn1_w_down[i])
        kind, j = i % N_MIXERS, i // N_MIXERS
        if kind == 0:
            x = _pool_mixer(x, mixer_norm[i], pool_w[j], pool_b[j], pool_scale[j])
        elif kind == 1:
            x = _fourier_mixer(x, mixer_norm[i], fourier_w[j], fourier_b[j])
        else:
            x = _attention_mixer(x, mixer_norm[i], attn_w_qkv[j], attn_q_norm[j],
                                 attn_k_norm[j], attn_w_o[j])
        x = ffn(x, ffn2_norm[i], ffn2_w_gate[i], ffn2_w_up[i], ffn2_w_down[i],
                final_norm if i == depth - 1 else None)
    return x
```

```python
import functools
import math

import jax
import jax.numpy as jnp
from jax import lax
from jax.experimental import pallas as pl
from jax.experimental.pallas import tpu as pltpu

D_MODEL = 1024
D_FF = 2816
N_MIXERS = 3
NORM_EPS = 1e-6
POOL_WINDOWS = (2, 4, 8, 16)
POOL_GROUP = D_MODEL // len(POOL_WINDOWS)
POOL_HALO = max(POOL_WINDOWS) // 2
N_FOURIER_GROUPS = 4
FOURIER_GROUP = D_MODEL // N_FOURIER_GROUPS
DFT_RADIX = 4
HEAD_DIM = 128
N_Q_HEADS = D_MODEL // HEAD_DIM
N_KV_HEADS = N_Q_HEADS // 4
Q_PER_KV = N_Q_HEADS // N_KV_HEADS
D_Q = N_Q_HEADS * HEAD_DIM
D_KV = N_KV_HEADS * HEAD_DIM
V_AUG_ROWS = HEAD_DIM + 16
GRID_W = 64
ROPE_THETA = 10000.0

V7X_MXU_DIM = 256
V7X_BF16_ROWS = 16
V7X_VMEM_BYTES = 64 * 1024 * 1024
VMEM_LIMIT_BYTES = V7X_VMEM_BYTES * 7 // 8

FFN_ROWS = 512
FFN_COLS = V7X_MXU_DIM
POOL_ROWS = 1024
POOL_SUB_ROWS = V7X_MXU_DIM
POOL_PAD = V7X_BF16_ROWS
TOKEN_ROWS = 512
CHAN_DFT_ROWS = 128
SEQ_DFT_SUB_ROWS = V7X_MXU_DIM
ATTN_Q_ROWS = 256
ATTN_KV_ROWS = 512

F32 = jnp.float32
BF16 = jnp.bfloat16


def _params(*semantics):
    return pltpu.CompilerParams(dimension_semantics=semantics,
                                vmem_limit_bytes=VMEM_LIMIT_BYTES)


def _resident(shape, index=None):
    lead = tuple(index if d is None else 0 for d in shape)
    return pl.BlockSpec(shape, lambda *_: lead, pipeline_mode=pl.Buffered(1))


def _rms(x, gain):
    ms = jnp.mean(x * x, axis=-1, keepdims=True)
    return x * lax.rsqrt(ms + NORM_EPS) * gain


def _dot(a, b):
    return jnp.dot(a, b, preferred_element_type=F32)


def _dot_bt(a, b):
    return lax.dot_general(a, b, (((1,), (1,)), ((), ())), preferred_element_type=F32)


def _ffn_body(*refs, final_norm):
    if final_norm:
        x_ref, g_ref, wg_ref, wu_ref, wd_ref, fg_ref, o_ref, act_ref = refs
    else:
        x_ref, g_ref, wg_ref, wu_ref, wd_ref, o_ref, act_ref = refs
    x = x_ref[...]
    h = _rms(x, g_ref[...]).astype(BF16)
    for c in range(D_FF // FFN_COLS):
        cols = slice(c * FFN_COLS, (c + 1) * FFN_COLS)
        gate = _dot(h, wg_ref[:, cols])
        up = _dot(h, wu_ref[:, cols])
        act_ref[:, cols] = (gate * jax.nn.sigmoid(gate) * up).astype(BF16)
    out = x + 0.5 * _dot(act_ref[...], wd_ref[...])
    if final_norm:
        out = _rms(out, fg_ref[...])
    o_ref[...] = out


def _ffn(x2d, layer, gains, wg, wu, wd, final_gain=None):
    n = x2d.shape[0]
    row_spec = pl.BlockSpec((FFN_ROWS, D_MODEL), lambda i: (i, 0))
    in_specs = [row_spec, _resident((None, 1, D_MODEL), layer),
                _resident((None, D_MODEL, D_FF), layer),
                _resident((None, D_MODEL, D_FF), layer),
                _resident((None, D_FF, D_MODEL), layer)]
    args = [x2d, gains, wg, wu, wd]
    if final_gain is not None:
        in_specs.append(_resident((1, D_MODEL)))
        args.append(final_gain.reshape(1, D_MODEL))
    return pl.pallas_call(
        functools.partial(_ffn_body, final_norm=final_gain is not None),
        out_shape=jax.ShapeDtypeStruct(x2d.shape, F32),
        grid=(n // FFN_ROWS,),
        in_specs=in_specs,
        out_specs=row_spec,
        scratch_shapes=[pltpu.VMEM((FFN_ROWS, D_FF), BF16)],
        compiler_params=_params("parallel"),
        name="ffn_final" if final_gain is not None else "ffn",
    )(*args)


def _pool_bands():
    t = jnp.arange(POOL_SUB_ROWS)[:, None]
    r = jnp.arange(POOL_SUB_ROWS + 2 * POOL_PAD)[None, :] - POOL_PAD
    bands = [((r >= t - w // 2) & (r < t + w // 2)) for w in POOL_WINDOWS]
    return jnp.stack(bands).astype(BF16)


def _pool_body(xm_ref, xp_ref, xn_ref, g_ref, band_ref, pw_ref, pb_ref, ps_ref,
               o_ref, ext_ref, *, seq):
    i = pl.program_id(1)
    gain = g_ref[...]
    fill = jnp.zeros((POOL_PAD - POOL_HALO, D_MODEL), F32)
    hp = jnp.where(i > 0, _rms(xp_ref[0], gain), 0.0)
    hn = jnp.where(i < pl.num_programs(1) - 1, _rms(xn_ref[0], gain), 0.0)
    ext_ref[0:POOL_PAD] = jnp.concatenate([fill, hp], axis=0)
    ext_ref[POOL_PAD:POOL_PAD + POOL_ROWS] = _rms(xm_ref[0], gain)
    ext_ref[POOL_PAD + POOL_ROWS:] = jnp.concatenate([hn, fill], axis=0)
    win_rows = POOL_SUB_ROWS + 2 * POOL_PAD
    sub_starts = range(0, POOL_ROWS, POOL_SUB_ROWS)
    for g, w in enumerate(POOL_WINDOWS):
        cols = slice(g * POOL_GROUP, (g + 1) * POOL_GROUP)
        terms = []
        for r0 in sub_starts:
            win = ext_ref[r0:r0 + win_rows, cols]
            win_hi = win.astype(BF16)
            terms += [win_hi, (win - win_hi.astype(F32)).astype(BF16)]
        wsums = _dot(band_ref[g], jnp.concatenate(terms, axis=1))
        pooled = []
        for k, r0 in enumerate(sub_starts):
            hi_cols = slice(2 * k * POOL_GROUP, (2 * k + 1) * POOL_GROUP)
            lo_cols = slice((2 * k + 1) * POOL_GROUP, (2 * k + 2) * POOL_GROUP)
            t = i * POOL_ROWS + r0 + lax.broadcasted_iota(jnp.int32, (POOL_SUB_ROWS, 1), 0)
            cnt = jnp.minimum(t + w // 2, seq) - jnp.maximum(t - w // 2, 0)
            hm = ext_ref[POOL_PAD + r0:POOL_PAD + r0 + POOL_SUB_ROWS, cols]
            pooled.append((wsums[:, hi_cols] + wsums[:, lo_cols]) / cnt.astype(F32) - hm)
        y = _dot(jnp.concatenate(pooled, axis=0).astype(BF16), pw_ref[g]) + pb_ref[g]
        o_ref[0, :, cols] = xm_ref[0, :, cols] + y * ps_ref[:, cols]


def _pool_mixer(x, gain, pw, pb, ps):
    b, s, _ = x.shape
    blocks_per_tile = POOL_ROWS // POOL_HALO
    n_halo_blocks = s // POOL_HALO
    n_groups = len(POOL_WINDOWS)
    return pl.pallas_call(
        functools.partial(_pool_body, seq=s),
        out_shape=jax.ShapeDtypeStruct(x.shape, F32),
        grid=(b, s // POOL_ROWS),
        in_specs=[
            pl.BlockSpec((1, POOL_ROWS, D_MODEL), lambda bi, i: (bi, i, 0)),
            pl.BlockSpec((1, POOL_HALO, D_MODEL),
                         lambda bi, i: (bi, jnp.maximum(i * blocks_per_tile - 1, 0), 0)),
            pl.BlockSpec((1, POOL_HALO, D_MODEL),
                         lambda bi, i: (bi, jnp.minimum((i + 1) * blocks_per_tile,
                                                        n_halo_blocks - 1), 0)),
            _resident((1, D_MODEL)),
            _resident((n_groups, POOL_SUB_ROWS, POOL_SUB_ROWS + 2 * POOL_PAD)),
            _resident((n_groups, POOL_GROUP, POOL_GROUP)),
            _resident((n_groups, 1, POOL_GROUP)),
            _resident((1, D_MODEL)),
        ],
        out_specs=pl.BlockSpec((1, POOL_ROWS, D_MODEL), lambda bi, i: (bi, i, 0)),
        scratch_shapes=[pltpu.VMEM((POOL_ROWS + 2 * POOL_PAD, D_MODEL), F32)],
        compiler_params=_params("parallel", "parallel"),
        name="pool_mixer",
    )(x, x, x, gain.reshape(1, D_MODEL), _pool_bands(), pw.astype(BF16),
      pb.reshape(n_groups, 1, POOL_GROUP), ps.reshape(1, D_MODEL))


def _dft_tables(n_rows, n_cols, n):
    k = (jnp.arange(n_rows, dtype=jnp.int32)[:, None]
         * jnp.arange(n_cols, dtype=jnp.int32)[None, :]) % n
    ang = k.astype(F32) * (2.0 * math.pi / n)
    return jnp.cos(ang), jnp.sin(ang)


def _seq_dft_matrix(s):
    quarter = s // DFT_RADIX
    root = int(math.isqrt(s))
    assert root * root == s
    ac, as_ = _dft_tables(root, root, root)
    ac = jnp.tile(ac, (1, quarter // root))
    as_ = jnp.tile(as_, (1, quarter // root))
    bc, bs = _dft_tables(root, quarter, s)
    cos = (ac[:, None, :] * bc[None, :, :] - as_[:, None, :] * bs[None, :, :]).reshape(s, quarter)
    sin = (as_[:, None, :] * bc[None, :, :] + ac[:, None, :] * bs[None, :, :]).reshape(s, quarter)
    w = jnp.concatenate([cos, sin], axis=1).astype(BF16)
    return w.reshape(quarter, DFT_RADIX * 2 * quarter)


def _chan_dft_body(x_ref, g_ref, cc_ref, nsc_ref, u_ref):
    gain = g_ref[...]
    z_re, z_im = [], []
    for q in range(DFT_RADIX):
        h = _rms(x_ref[0, q], gain).astype(BF16)
        re, im = [], []
        for g in range(N_FOURIER_GROUPS):
            cols = slice(g * FOURIER_GROUP, (g + 1) * FOURIER_GROUP)
            re.append(_dot(h[:, cols], cc_ref[...]))
            im.append(_dot(h[:, cols], nsc_ref[...]))
        z_re.append(jnp.concatenate(re, axis=-1))
        z_im.append(jnp.concatenate(im, axis=-1))
    t0 = (z_re[0] + z_re[2], z_im[0] + z_im[2])
    t1 = (z_re[0] - z_re[2], z_im[0] - z_im[2])
    t2 = (z_re[1] + z_re[3], z_im[1] + z_im[3])
    t3 = (z_re[1] - z_re[3], z_im[1] - z_im[3])
    u = [(t0[0] + t2[0], t0[1] + t2[1]),
         (t1[0] + t3[1], t1[1] - t3[0]),
         (t0[0] - t2[0], t0[1] - t2[1]),
         (t1[0] - t3[1], t1[1] + t3[0])]
    for r in range(DFT_RADIX):
        u_ref[0, r, 0] = u[r][0].astype(BF16)
        u_ref[0, r, 1] = u[r][1].astype(BF16)


def _seq_dft_body(w_ref, u_ref, x_ref, wo_ref, bo_ref, o_ref, *, inv_norm):
    for rt in range(w_ref.shape[0] // SEQ_DFT_SUB_ROWS):
        rows = slice(rt * SEQ_DFT_SUB_ROWS, (rt + 1) * SEQ_DFT_SUB_ROWS)
        f = _dot(w_ref[rows, :], u_ref[0, 0]) * inv_norm
        o_ref[0, rows, :] = x_ref[0, rows, :] + _dot(f.astype(BF16), wo_ref[...]) + bo_ref[...]


def _fourier_mixer(x, gain, w_out, b_out):
    b, s, _ = x.shape
    quarter = s // DFT_RADIX
    cc, sc = _dft_tables(FOURIER_GROUP, FOURIER_GROUP, FOURIER_GROUP)
    u = pl.pallas_call(
        _chan_dft_body,
        out_shape=jax.ShapeDtypeStruct((b, DFT_RADIX, 2, quarter, D_MODEL), BF16),
        grid=(b, quarter // CHAN_DFT_ROWS),
        in_specs=[pl.BlockSpec((1, DFT_RADIX, CHAN_DFT_ROWS, D_MODEL),
                               lambda bi, i: (bi, 0, i, 0)),
                  _resident((1, D_MODEL)),
                  _resident((FOURIER_GROUP, FOURIER_GROUP)),
                  _resident((FOURIER_GROUP, FOURIER_GROUP))],
        out_specs=pl.BlockSpec((1, DFT_RADIX, 2, CHAN_DFT_ROWS, D_MODEL),
                               lambda bi, i: (bi, 0, 0, i, 0)),
        compiler_params=_params("parallel", "parallel"),
        name="fourier_chan_dft",
    )(x.reshape(b, DFT_RADIX, quarter, D_MODEL), gain.reshape(1, D_MODEL),
      cc.astype(BF16), (-sc).astype(BF16))
    u = u.reshape(b, DFT_RADIX, 2 * quarter, D_MODEL)
    x4 = x.reshape(b, quarter, DFT_RADIX * D_MODEL)
    inv_norm = 1.0 / math.sqrt(s * FOURIER_GROUP)
    out = pl.pallas_call(
        functools.partial(_seq_dft_body, inv_norm=inv_norm),
        out_shape=jax.ShapeDtypeStruct(x4.shape, F32),
        grid=(DFT_RADIX, b),
        in_specs=[pl.BlockSpec((quarter, 2 * quarter), lambda r, bi: (0, r)),
                  pl.BlockSpec((1, 1, 2 * quarter, D_MODEL), lambda r, bi: (bi, r, 0, 0)),
                  pl.BlockSpec((1, quarter, D_MODEL), lambda r, bi: (bi, 0, r)),
                  _resident((D_MODEL, D_MODEL)),
                  _resident((1, D_MODEL))],
        out_specs=pl.BlockSpec((1, quarter, D_MODEL), lambda r, bi: (bi, 0, r)),
        compiler_params=_params("parallel", "parallel"),
        name="fourier_seq_dft",
    )(_seq_dft_matrix(s), u, x4, w_out.astype(BF16), b_out.reshape(1, D_MODEL))
    return out.reshape(b, s, D_MODEL)


def _rope_angles(s):
    rows = s // GRID_W
    row = jnp.repeat(jnp.arange(rows, dtype=F32), GRID_W)
    col = jnp.tile(jnp.arange(GRID_W, dtype=F32), rows)
    half = HEAD_DIM // 2
    inv_freq = ROPE_THETA ** (-jnp.arange(0, half, 2, dtype=F32) / half)
    ang_r = row[:, None] * inv_freq[None, :]
    ang_c = col[:, None] * inv_freq[None, :]
    return jnp.concatenate([ang_r, ang_r, ang_c, ang_c], axis=-1)


def _swap_quarters(a, axis):
    q0, q1, q2, q3 = jnp.split(a, 4, axis=axis)
    return jnp.concatenate([q1, q0, q3, q2], axis=axis)


def _qkv_body(x_ref, g_ref, wqv_ref, wk_ref, kg_ref, qcos_ref, qsin_ref,
              kcos_ref, ksup_ref, ksdn_ref, qt_ref, k_ref, vt_ref):
    h = _rms(x_ref[...], g_ref[...]).astype(BF16)
    rows = h.shape[0]
    k2 = _dot(h, wk_ref[...])
    kcos, ksup, ksdn = kcos_ref[...], ksup_ref[...], ksdn_ref[...]
    quarter = HEAD_DIM // 4
    for g in range(N_KV_HEADS):
        u = k2[:, g * HEAD_DIM:(g + 1) * HEAD_DIM]
        ms = jnp.mean(u * u, axis=-1, keepdims=True)
        u = u * lax.rsqrt(ms + NORM_EPS) * kg_ref[...]
        up = pltpu.roll(u, HEAD_DIM - quarter, 1)
        dn = pltpu.roll(u, quarter, 1)
        k_ref[:, g * HEAD_DIM:(g + 1) * HEAD_DIM] = (
            u * kcos + (up * ksup + dn * ksdn)).astype(BF16)
    ut = _dot_bt(wqv_ref[...], h)
    qcos, qsin = qcos_ref[...], qsin_ref[...]
    for hd in range(N_Q_HEADS):
        u = ut[hd * HEAD_DIM:(hd + 1) * HEAD_DIM, :]
        ms = jnp.mean(u * u, axis=0, keepdims=True)
        n = u * lax.rsqrt(ms + NORM_EPS)
        r = n * qcos + _swap_quarters(n, 0) * qsin
        qt_ref[hd * HEAD_DIM:(hd + 1) * HEAD_DIM, :] = r.astype(BF16)
    ones_row = lax.broadcasted_iota(jnp.int32, (V_AUG_ROWS - HEAD_DIM, rows), 0) == 0
    for g in range(N_KV_HEADS):
        v_t = ut[D_Q + g * HEAD_DIM:D_Q + (g + 1) * HEAD_DIM, :]
        vt_ref[g, 0:HEAD_DIM, :] = v_t.astype(BF16)
        vt_ref[g, HEAD_DIM:, :] = jnp.where(ones_row, 1.0, 0.0).astype(BF16)


def _attn_body(qt_ref, k_ref, vt_ref, o_ref, s_ref, *, seq):
    n_chunks = seq // ATTN_KV_ROWS

    def scores(c, qcat):
        st = _dot(k_ref[c * ATTN_KV_ROWS:(c + 1) * ATTN_KV_ROWS, :], qcat)
        s_ref[c % 2] = st
        return jnp.max(st, axis=0, keepdims=True)

    def q_block(i, carry):
        r0 = pl.multiple_of(i * ATTN_Q_ROWS, ATTN_Q_ROWS)
        qb = qt_ref[:, pl.ds(r0, ATTN_Q_ROWS)]
        qcat = jnp.concatenate([qb[r * HEAD_DIM:(r + 1) * HEAD_DIM, :]
                                for r in range(Q_PER_KV)], axis=1)
        m = acc = None
        cm = scores(0, qcat)
        for c in range(n_chunks):
            cm_next = scores(c + 1, qcat) if c + 1 < n_chunks else None
            m_new = cm if m is None else jnp.maximum(m, cm)
            p = jnp.exp2(s_ref[c % 2] - m_new).astype(BF16)
            pv = _dot(vt_ref[0, :, c * ATTN_KV_ROWS:(c + 1) * ATTN_KV_ROWS], p)
            acc = pv if m is None else acc * jnp.exp2(m - m_new) + pv
            m, cm = m_new, cm_next
        out_t = acc[0:HEAD_DIM] / acc[HEAD_DIM:HEAD_DIM + 1]
        for r in range(Q_PER_KV):
            head = out_t[:, r * ATTN_Q_ROWS:(r + 1) * ATTN_Q_ROWS].T
            o_ref[0, pl.ds(r0, ATTN_Q_ROWS), r * HEAD_DIM:(r + 1) * HEAD_DIM] = head.astype(BF16)
        return carry

    lax.fori_loop(0, seq // ATTN_Q_ROWS, q_block, 0)


def _proj_body(a_ref, x_ref, w_ref, o_ref):
    o_ref[...] = x_ref[...] + _dot(a_ref[...], w_ref[...])


def _attention_mixer(x, gain, w_qkv, q_gain, k_gain, w_o):
    b, s, _ = x.shape
    n = b * s
    x2d = x.reshape(n, D_MODEL)
    ang = _rope_angles(s)
    cos, sin = jnp.cos(ang), jnp.sin(ang)
    first = ((jnp.arange(HEAD_DIM) % (HEAD_DIM // 2)) < HEAD_DIM // 4)
    ksup = jnp.where(first[None, :], -sin, 0.0)
    ksdn = jnp.where(first[None, :], 0.0, sin)
    c = HEAD_DIM ** -0.5 * math.log2(math.e)
    sign = jnp.where(first, -1.0, 1.0)
    qcos = (q_gain * c)[:, None] * cos.T
    qsin = (_swap_quarters(q_gain, 0) * sign * c)[:, None] * sin.T
    wqv_t = jnp.concatenate([w_qkv[:, :D_Q], w_qkv[:, D_Q + D_KV:]], axis=1).T.astype(BF16)
    wk = w_qkv[:, D_Q:D_Q + D_KV].astype(BF16)
    tiles_per_seq = s // TOKEN_ROWS
    row = lambda width: pl.BlockSpec((TOKEN_ROWS, width), lambda i: (i, 0))
    pos = pl.BlockSpec((TOKEN_ROWS, HEAD_DIM), lambda i: (i % tiles_per_seq, 0))
    pos_t = pl.BlockSpec((HEAD_DIM, TOKEN_ROWS), lambda i: (0, i % tiles_per_seq))
    qt, k, vt = pl.pallas_call(
        _qkv_body,
        out_shape=(jax.ShapeDtypeStruct((D_Q, n), BF16),
                   jax.ShapeDtypeStruct((n, D_KV), BF16),
                   jax.ShapeDtypeStruct((N_KV_HEADS, V_AUG_ROWS, n), BF16)),
        grid=(n // TOKEN_ROWS,),
        in_specs=[row(D_MODEL), _resident((1, D_MODEL)),
                  _resident((D_Q + D_KV, D_MODEL)), _resident((D_MODEL, D_KV)),
                  _resident((1, HEAD_DIM)), pos_t, pos_t, pos, pos, pos],
        out_specs=(pl.BlockSpec((D_Q, TOKEN_ROWS), lambda i: (0, i)), row(D_KV),
                   pl.BlockSpec((N_KV_HEADS, V_AUG_ROWS, TOKEN_ROWS), lambda i: (0, 0, i))),
        compiler_params=_params("parallel"),
        name="attn_qkv",
    )(x2d, gain.reshape(1, D_MODEL), wqv_t, wk, k_gain.reshape(1, HEAD_DIM),
      qcos, qsin, cos, ksup, ksdn)

    group_w = Q_PER_KV * HEAD_DIM
    attn = pl.pallas_call(
        functools.partial(_attn_body, seq=s),
        out_shape=jax.ShapeDtypeStruct((b, s, D_Q), BF16),
        grid=(b, N_KV_HEADS),
        in_specs=[pl.BlockSpec((group_w, s), lambda bi, g: (g, bi)),
                  pl.BlockSpec((s, HEAD_DIM), lambda bi, g: (bi, g)),
                  pl.BlockSpec((1, V_AUG_ROWS, s), lambda bi, g: (g, 0, bi))],
        out_specs=pl.BlockSpec((1, s, group_w), lambda bi, g: (bi, 0, g)),
        scratch_shapes=[pltpu.VMEM((2, ATTN_KV_ROWS, Q_PER_KV * ATTN_Q_ROWS), F32)],
        compiler_params=_params("parallel", "parallel"),
        name="attn_core",
    )(qt, k, vt)

    out = pl.pallas_call(
        _proj_body,
        out_shape=jax.ShapeDtypeStruct((n, D_MODEL), F32),
        grid=(n // TOKEN_ROWS,),
        in_specs=[row(D_Q), row(D_MODEL), _resident((D_Q, D_MODEL))],
        out_specs=row(D_MODEL),
        compiler_params=_params("parallel"),
        name="attn_out_proj",
    )(attn.reshape(n, D_Q), x2d, w_o.astype(BF16))
    return out.reshape(b, s, D_MODEL)


def kernel(x, ffn1_norm, ffn1_w_gate, ffn1_w_up, ffn1_w_down, mixer_norm, ffn2_norm, ffn2_w_gate, ffn2_w_up, ffn2_w_down, pool_w, pool_b, pool_scale, fourier_w, fourier_b, attn_w_qkv, attn_q_norm, attn_k_norm, attn_w_o, final_norm):
    b, s, d = x.shape
    depth = ffn1_norm.shape[0]
    n = b * s
    ffn1 = (ffn1_norm.reshape(depth, 1, d), ffn1_w_gate.astype(BF16),
            ffn1_w_up.astype(BF16), ffn1_w_down.astype(BF16))
    ffn2 = (ffn2_norm.reshape(depth, 1, d), ffn2_w_gate.astype(BF16),
            ffn2_w_up.astype(BF16), ffn2_w_down.astype(BF16))

    def ffn(x3d, layer, params, final_gain=None):
        return _ffn(x3d.reshape(n, d), layer, *params, final_gain=final_gain).reshape(b, s, d)

    for i in range(depth):
        x = ffn(x, i, ffn1)
        kind, j = i % N_MIXERS, i // N_MIXERS
        if kind == 0:
            x = _pool_mixer(x, mixer_norm[i], pool_w[j], pool_b[j], pool_scale[j])
        elif kind == 1:
            x = _fourier_mixer(x, mixer_norm[i], fourier_w[j], fourier_b[j])
        else:
            x = _attention_mixer(x, mixer_norm[i], attn_w_qkv[j], attn_q_norm[j],
                                 attn_k_norm[j], attn_w_o[j])
        x = ffn(x, i, ffn2, final_norm if i == depth - 1 else None)
    return x
```

```python
import functools
import math

import jax
import jax.numpy as jnp
from jax import lax
from jax.experimental import pallas as pl
from jax.experimental.pallas import tpu as pltpu

D_MODEL = 1024
D_FF = 2816
N_MIXERS = 3
NORM_EPS = 1e-6
POOL_WINDOWS = (2, 4, 8, 16)
POOL_GROUP = D_MODEL // len(POOL_WINDOWS)
POOL_HALO = max(POOL_WINDOWS) // 2
N_FOURIER_GROUPS = 4
FOURIER_GROUP = D_MODEL // N_FOURIER_GROUPS
DFT_RADIX = 4
HEAD_DIM = 128
N_Q_HEADS = D_MODEL // HEAD_DIM
N_KV_HEADS = N_Q_HEADS // 4
Q_PER_KV = N_Q_HEADS // N_KV_HEADS
D_Q = N_Q_HEADS * HEAD_DIM
D_KV = N_KV_HEADS * HEAD_DIM
V_AUG_ROWS = HEAD_DIM + 16
GRID_W = 64
ROPE_THETA = 10000.0

V7X_MXU_DIM = 256
V7X_BF16_ROWS = 16
V7X_VMEM_BYTES = 64 * 1024 * 1024
VMEM_LIMIT_BYTES = V7X_VMEM_BYTES * 7 // 8

FFN_ROWS = 1024
FFN_COLS = V7X_MXU_DIM
POOL_ROWS = 1024
POOL_SUB_ROWS = V7X_MXU_DIM
POOL_PAD = V7X_BF16_ROWS
TOKEN_ROWS = 512
CHAN_DFT_ROWS = 128
SEQ_DFT_ROWS = V7X_MXU_DIM
SEQ_DFT_GROUP = V7X_MXU_DIM // DFT_RADIX
ATTN_Q_ROWS = 256
ATTN_KV_ROWS = 512

F32 = jnp.float32
BF16 = jnp.bfloat16


def _params(*semantics):
    return pltpu.CompilerParams(dimension_semantics=semantics,
                                vmem_limit_bytes=VMEM_LIMIT_BYTES)


def _resident(shape, index=None):
    lead = tuple(index if d is None else 0 for d in shape)
    return pl.BlockSpec(shape, lambda *_: lead, pipeline_mode=pl.Buffered(1))


def _rms(x, gain):
    ms = jnp.mean(x * x, axis=-1, keepdims=True)
    return x * lax.rsqrt(ms + NORM_EPS) * gain


def _dot(a, b):
    return jnp.dot(a, b, preferred_element_type=F32)


def _dot_bt(a, b):
    return lax.dot_general(a, b, (((1,), (1,)), ((), ())), preferred_element_type=F32)


def _ffn_body(*refs, final_norm, pre_proj):
    refs = list(refs)
    x_ref = refs.pop(0)
    x = x_ref[...]
    if pre_proj:
        a_ref, wp_ref = refs.pop(0), refs.pop(0)
        x = x + _dot(a_ref[...], wp_ref[...])
    g_ref, wg_ref, wu_ref, wd_ref = refs[:4]
    o_ref, act_ref = refs[-2:]
    h = _rms(x, g_ref[...]).astype(BF16)
    for c in range(D_FF // FFN_COLS):
        cols = slice(c * FFN_COLS, (c + 1) * FFN_COLS)
        gate = _dot(h, wg_ref[:, cols])
        up = _dot(h, wu_ref[:, cols])
        act_ref[:, cols] = (gate * jax.nn.sigmoid(gate) * up).astype(BF16)
    out = x + 0.5 * _dot(act_ref[...], wd_ref[...])
    if final_norm:
        out = _rms(out, refs[4][...])
    o_ref[...] = out


def _ffn(x2d, layer, gains, wg, wu, wd, final_gain=None, proj=None):
    n = x2d.shape[0]
    row_spec = pl.BlockSpec((FFN_ROWS, D_MODEL), lambda i: (i, 0))
    in_specs, args = [row_spec], [x2d]
    if proj is not None:
        a, w_proj = proj
        in_specs += [pl.BlockSpec((FFN_ROWS, a.shape[1]), lambda i: (i, 0)),
                     _resident(w_proj.shape)]
        args += [a, w_proj]
    in_specs += [_resident((None, 1, D_MODEL), layer),
                 _resident((None, D_MODEL, D_FF), layer),
                 _resident((None, D_MODEL, D_FF), layer),
                 _resident((None, D_FF, D_MODEL), layer)]
    args += [gains, wg, wu, wd]
    if final_gain is not None:
        in_specs.append(_resident((1, D_MODEL)))
        args.append(final_gain.reshape(1, D_MODEL))
    name = "ffn" + ("_proj" if proj is not None else "") + ("_final" if final_gain is not None else "")
    return pl.pallas_call(
        functools.partial(_ffn_body, final_norm=final_gain is not None,
                          pre_proj=proj is not None),
        out_shape=jax.ShapeDtypeStruct(x2d.shape, F32),
        grid=(n // FFN_ROWS,),
        in_specs=in_specs,
        out_specs=row_spec,
        scratch_shapes=[pltpu.VMEM((FFN_ROWS, D_FF), BF16)],
        compiler_params=_params("parallel"),
        name=name,
    )(*args)


def _pool_bands():
    t = jnp.arange(POOL_SUB_ROWS)[:, None]
    r = jnp.arange(POOL_SUB_ROWS + 2 * POOL_PAD)[None, :] - POOL_PAD
    bands = [((r >= t - w // 2) & (r < t + w // 2)) for w in POOL_WINDOWS]
    return jnp.stack(bands).astype(BF16)


def _pool_body(xm_ref, xp_ref, xn_ref, g_ref, band_ref, pw_ref, pb_ref, ps_ref,
               o_ref, ext_ref, *, seq):
    i = pl.program_id(1)
    gain = g_ref[...]
    fill = jnp.zeros((POOL_PAD - POOL_HALO, D_MODEL), F32)
    hp = jnp.where(i > 0, _rms(xp_ref[0], gain), 0.0)
    hn = jnp.where(i < pl.num_programs(1) - 1, _rms(xn_ref[0], gain), 0.0)
    ext_ref[0:POOL_PAD] = jnp.concatenate([fill, hp], axis=0)
    ext_ref[POOL_PAD:POOL_PAD + POOL_ROWS] = _rms(xm_ref[0], gain)
    ext_ref[POOL_PAD + POOL_ROWS:] = jnp.concatenate([hn, fill], axis=0)
    win_rows = POOL_SUB_ROWS + 2 * POOL_PAD
    sub_starts = range(0, POOL_ROWS, POOL_SUB_ROWS)
    for g, w in enumerate(POOL_WINDOWS):
        cols = slice(g * POOL_GROUP, (g + 1) * POOL_GROUP)
        terms = []
        for r0 in sub_starts:
            win = ext_ref[r0:r0 + win_rows, cols]
            win_hi = win.astype(BF16)
            terms += [win_hi, (win - win_hi.astype(F32)).astype(BF16)]
        wsums = _dot(band_ref[g], jnp.concatenate(terms, axis=1))
        pooled = []
        for k, r0 in enumerate(sub_starts):
            hi_cols = slice(2 * k * POOL_GROUP, (2 * k + 1) * POOL_GROUP)
            lo_cols = slice((2 * k + 1) * POOL_GROUP, (2 * k + 2) * POOL_GROUP)
            t = i * POOL_ROWS + r0 + lax.broadcasted_iota(jnp.int32, (POOL_SUB_ROWS, 1), 0)
            cnt = jnp.minimum(t + w // 2, seq) - jnp.maximum(t - w // 2, 0)
            hm = ext_ref[POOL_PAD + r0:POOL_PAD + r0 + POOL_SUB_ROWS, cols]
            pooled.append((wsums[:, hi_cols] + wsums[:, lo_cols]) / cnt.astype(F32) - hm)
        y = _dot(jnp.concatenate(pooled, axis=0).astype(BF16), pw_ref[g]) + pb_ref[g]
        o_ref[0, :, cols] = xm_ref[0, :, cols] + y * ps_ref[:, cols]


def _pool_mixer(x, gain, pw, pb, ps):
    b, s, _ = x.shape
    blocks_per_tile = POOL_ROWS // POOL_HALO
    n_halo_blocks = s // POOL_HALO
    n_groups = len(POOL_WINDOWS)
    return pl.pallas_call(
        functools.partial(_pool_body, seq=s),
        out_shape=jax.ShapeDtypeStruct(x.shape, F32),
        grid=(b, s // POOL_ROWS),
        in_specs=[
            pl.BlockSpec((1, POOL_ROWS, D_MODEL), lambda bi, i: (bi, i, 0)),
            pl.BlockSpec((1, POOL_HALO, D_MODEL),
                         lambda bi, i: (bi, jnp.maximum(i * blocks_per_tile - 1, 0), 0)),
            pl.BlockSpec((1, POOL_HALO, D_MODEL),
                         lambda bi, i: (bi, jnp.minimum((i + 1) * blocks_per_tile,
                                                        n_halo_blocks - 1), 0)),
            _resident((1, D_MODEL)),
            _resident((n_groups, POOL_SUB_ROWS, POOL_SUB_ROWS + 2 * POOL_PAD)),
            _resident((n_groups, POOL_GROUP, POOL_GROUP)),
            _resident((n_groups, 1, POOL_GROUP)),
            _resident((1, D_MODEL)),
        ],
        out_specs=pl.BlockSpec((1, POOL_ROWS, D_MODEL), lambda bi, i: (bi, i, 0)),
        scratch_shapes=[pltpu.VMEM((POOL_ROWS + 2 * POOL_PAD, D_MODEL), F32)],
        compiler_params=_params("parallel", "parallel"),
        name="pool_mixer",
    )(x, x, x, gain.reshape(1, D_MODEL), _pool_bands(), pw.astype(BF16),
      pb.reshape(n_groups, 1, POOL_GROUP), ps.reshape(1, D_MODEL))


def _dft_tables(n_rows, n_cols, n):
    k = (jnp.arange(n_rows, dtype=jnp.int32)[:, None]
         * jnp.arange(n_cols, dtype=jnp.int32)[None, :]) % n
    ang = k.astype(F32) * (2.0 * math.pi / n)
    return jnp.cos(ang), jnp.sin(ang)


def _seq_dft_matrix(s):
    quarter = s // DFT_RADIX
    ca, sa = _dft_tables(quarter, quarter, quarter)
    j = jnp.arange(DFT_RADIX * 2 * quarter, dtype=jnp.int32)
    r, part, t = j // (2 * quarter), (j // quarter) % 2, j % quarter
    beta = ((t * r) % s).astype(F32) * (2.0 * math.pi / s) - part.astype(F32) * (math.pi / 2)
    reps = (1, DFT_RADIX * 2)
    w = (jnp.tile(ca, reps) * jnp.cos(beta)[None, :] - jnp.tile(sa, reps) * jnp.sin(beta)[None, :])
    return w.astype(BF16)


def _chan_dft_body(x_ref, g_ref, cc_ref, nsc_ref, u_ref):
    gain = g_ref[...]
    z_re, z_im = [], []
    for q in range(DFT_RADIX):
        h = _rms(x_ref[0, q], gain).astype(BF16)
        re, im = [], []
        for g in range(N_FOURIER_GROUPS):
            cols = slice(g * FOURIER_GROUP, (g + 1) * FOURIER_GROUP)
            re.append(_dot(h[:, cols], cc_ref[...]))
            im.append(_dot(h[:, cols], nsc_ref[...]))
        z_re.append(jnp.concatenate(re, axis=-1))
        z_im.append(jnp.concatenate(im, axis=-1))
    t0 = (z_re[0] + z_re[2], z_im[0] + z_im[2])
    t1 = (z_re[0] - z_re[2], z_im[0] - z_im[2])
    t2 = (z_re[1] + z_re[3], z_im[1] + z_im[3])
    t3 = (z_re[1] - z_re[3], z_im[1] - z_im[3])
    u = [(t0[0] + t2[0], t0[1] + t2[1]),
         (t1[0] + t3[1], t1[1] - t3[0]),
         (t0[0] - t2[0], t0[1] - t2[1]),
         (t1[0] - t3[1], t1[1] + t3[0])]
    for r in range(DFT_RADIX):
        u_ref[0, r, 0] = u[r][0].astype(BF16)
        u_ref[0, r, 1] = u[r][1].astype(BF16)


def _interleave_matrix():
    out_row = jnp.arange(DFT_RADIX * SEQ_DFT_GROUP)[:, None]
    in_row = jnp.arange(DFT_RADIX * SEQ_DFT_GROUP)[None, :]
    src = (out_row % DFT_RADIX) * SEQ_DFT_GROUP + out_row // DFT_RADIX
    return (in_row == src).astype(BF16)


def _seq_dft_body(w_ref, u_ref, x_ref, wo_ref, bo_ref, perm_ref, o_ref, *, inv_norm):
    rows, two_q = w_ref.shape[0], u_ref.shape[2]
    f = [(_dot(w_ref[:, r * two_q:(r + 1) * two_q], u_ref[0, r]) * inv_norm).astype(BF16)
         for r in range(DFT_RADIX)]
    span = DFT_RADIX * SEQ_DFT_GROUP
    for j in range(rows // SEQ_DFT_GROUP):
        grp = slice(j * SEQ_DFT_GROUP, (j + 1) * SEQ_DFT_GROUP)
        stacked = jnp.concatenate([f[r][grp] for r in range(DFT_RADIX)], axis=0)
        f_tok = _dot(perm_ref[...], stacked).astype(BF16)
        tok = slice(j * span, (j + 1) * span)
        o_ref[0, tok, :] = x_ref[0, tok, :] + _dot(f_tok, wo_ref[...]) + bo_ref[...]


def _fourier_mixer(x, gain, w_out, b_out):
    b, s, _ = x.shape
    quarter = s // DFT_RADIX
    cc, sc = _dft_tables(FOURIER_GROUP, FOURIER_GROUP, FOURIER_GROUP)
    u = pl.pallas_call(
        _chan_dft_body,
        out_shape=jax.ShapeDtypeStruct((b, DFT_RADIX, 2, quarter, D_MODEL), BF16),
        grid=(b, quarter // CHAN_DFT_ROWS),
        in_specs=[pl.BlockSpec((1, DFT_RADIX, CHAN_DFT_ROWS, D_MODEL),
                               lambda bi, i: (bi, 0, i, 0)),
                  _resident((1, D_MODEL)),
                  _resident((FOURIER_GROUP, FOURIER_GROUP)),
                  _resident((FOURIER_GROUP, FOURIER_GROUP))],
        out_specs=pl.BlockSpec((1, DFT_RADIX, 2, CHAN_DFT_ROWS, D_MODEL),
                               lambda bi, i: (bi, 0, 0, i, 0)),
        compiler_params=_params("parallel", "parallel"),
        name="fourier_chan_dft",
    )(x.reshape(b, DFT_RADIX, quarter, D_MODEL), gain.reshape(1, D_MODEL),
      cc.astype(BF16), (-sc).astype(BF16))
    u = u.reshape(b, DFT_RADIX, 2 * quarter, D_MODEL)
    inv_norm = 1.0 / math.sqrt(s * FOURIER_GROUP)
    tile = pl.BlockSpec((1, DFT_RADIX * SEQ_DFT_ROWS, D_MODEL), lambda bi, i: (bi, i, 0))
    return pl.pallas_call(
        functools.partial(_seq_dft_body, inv_norm=inv_norm),
        out_shape=jax.ShapeDtypeStruct(x.shape, F32),
        grid=(b, quarter // SEQ_DFT_ROWS),
        in_specs=[pl.BlockSpec((SEQ_DFT_ROWS, DFT_RADIX * 2 * quarter), lambda bi, i: (i, 0)),
                  pl.BlockSpec((1, DFT_RADIX, 2 * quarter, D_MODEL),
                               lambda bi, i: (bi, 0, 0, 0), pipeline_mode=pl.Buffered(1)),
                  tile,
                  _resident((D_MODEL, D_MODEL)),
                  _resident((1, D_MODEL)),
                  _resident((DFT_RADIX * SEQ_DFT_GROUP, DFT_RADIX * SEQ_DFT_GROUP))],
        out_specs=tile,
        compiler_params=_params("parallel", "arbitrary"),
        name="fourier_seq_dft",
    )(_seq_dft_matrix(s), u, x, w_out.astype(BF16), b_out.reshape(1, D_MODEL),
      _interleave_matrix())


def _rope_angles(s):
    rows = s // GRID_W
    row = jnp.repeat(jnp.arange(rows, dtype=F32), GRID_W)
    col = jnp.tile(jnp.arange(GRID_W, dtype=F32), rows)
    half = HEAD_DIM // 2
    inv_freq = ROPE_THETA ** (-jnp.arange(0, half, 2, dtype=F32) / half)
    ang_r = row[:, None] * inv_freq[None, :]
    ang_c = col[:, None] * inv_freq[None, :]
    return jnp.concatenate([ang_r, ang_r, ang_c, ang_c], axis=-1)


def _swap_quarters(a, axis):
    q0, q1, q2, q3 = jnp.split(a, 4, axis=axis)
    return jnp.concatenate([q1, q0, q3, q2], axis=axis)


def _qkv_body(x_ref, g_ref, wqv_ref, wk_ref, kg_ref, qcos_ref, qsin_ref,
              kcos_ref, ksup_ref, ksdn_ref, qt_ref, k_ref, vt_ref):
    h = _rms(x_ref[...], g_ref[...]).astype(BF16)
    rows = h.shape[0]
    k2 = _dot(h, wk_ref[...])
    kcos, ksup, ksdn = kcos_ref[...], ksup_ref[...], ksdn_ref[...]
    quarter = HEAD_DIM // 4
    for g in range(N_KV_HEADS):
        u = k2[:, g * HEAD_DIM:(g + 1) * HEAD_DIM]
        ms = jnp.mean(u * u, axis=-1, keepdims=True)
        u = u * lax.rsqrt(ms + NORM_EPS) * kg_ref[...]
        up = pltpu.roll(u, HEAD_DIM - quarter, 1)
        dn = pltpu.roll(u, quarter, 1)
        k_ref[:, g * HEAD_DIM:(g + 1) * HEAD_DIM] = (
            u * kcos + (up * ksup + dn * ksdn)).astype(BF16)
    ut = _dot_bt(wqv_ref[...], h)
    qcos, qsin = qcos_ref[...], qsin_ref[...]
    for hd in range(N_Q_HEADS):
        u = ut[hd * HEAD_DIM:(hd + 1) * HEAD_DIM, :]
        ms = jnp.mean(u * u, axis=0, keepdims=True)
        n = u * lax.rsqrt(ms + NORM_EPS)
        r = n * qcos + _swap_quarters(n, 0) * qsin
        qt_ref[hd * HEAD_DIM:(hd + 1) * HEAD_DIM, :] = r.astype(BF16)
    ones_row = lax.broadcasted_iota(jnp.int32, (V_AUG_ROWS - HEAD_DIM, rows), 0) == 0
    for g in range(N_KV_HEADS):
        v_t = ut[D_Q + g * HEAD_DIM:D_Q + (g + 1) * HEAD_DIM, :]
        vt_ref[g, 0:HEAD_DIM, :] = v_t.astype(BF16)
        vt_ref[g, HEAD_DIM:, :] = jnp.where(ones_row, 1.0, 0.0).astype(BF16)


def _attn_body(qt_ref, k_ref, vt_ref, o_ref, s_ref, *, seq):
    n_chunks = seq // ATTN_KV_ROWS

    def scores(c, qcat):
        st = _dot(k_ref[c * ATTN_KV_ROWS:(c + 1) * ATTN_KV_ROWS, :], qcat)
        s_ref[c % 2] = st
        return jnp.max(st, axis=0, keepdims=True)

    def q_block(i, carry):
        r0 = pl.multiple_of(i * ATTN_Q_ROWS, ATTN_Q_ROWS)
        qb = qt_ref[:, pl.ds(r0, ATTN_Q_ROWS)]
        qcat = jnp.concatenate([qb[r * HEAD_DIM:(r + 1) * HEAD_DIM, :]
                                for r in range(Q_PER_KV)], axis=1)
        m = acc = None
        cm = scores(0, qcat)
        for c in range(n_chunks):
            cm_next = scores(c + 1, qcat) if c + 1 < n_chunks else None
            m_new = cm if m is None else jnp.maximum(m, cm)
            p = jnp.exp2(s_ref[c % 2] - m_new).astype(BF16)
            pv = _dot(vt_ref[0, :, c * ATTN_KV_ROWS:(c + 1) * ATTN_KV_ROWS], p)
            acc = pv if m is None else acc * jnp.exp2(m - m_new) + pv
            m, cm = m_new, cm_next
        out_t = acc[0:HEAD_DIM] / acc[HEAD_DIM:HEAD_DIM + 1]
        for r in range(Q_PER_KV):
            head = out_t[:, r * ATTN_Q_ROWS:(r + 1) * ATTN_Q_ROWS].T
            o_ref[0, pl.ds(r0, ATTN_Q_ROWS), r * HEAD_DIM:(r + 1) * HEAD_DIM] = head.astype(BF16)
        return carry

    lax.fori_loop(0, seq // ATTN_Q_ROWS, q_block, 0)


def _attention_mixer(x, gain, w_qkv, q_gain, k_gain, w_o):
    b, s, _ = x.shape
    n = b * s
    x2d = x.reshape(n, D_MODEL)
    ang = _rope_angles(s)
    cos, sin = jnp.cos(ang), jnp.sin(ang)
    first = ((jnp.arange(HEAD_DIM) % (HEAD_DIM // 2)) < HEAD_DIM // 4)
    ksup = jnp.where(first[None, :], -sin, 0.0)
    ksdn = jnp.where(first[None, :], 0.0, sin)
    c = HEAD_DIM ** -0.5 * math.log2(math.e)
    sign = jnp.where(first, -1.0, 1.0)
    qcos = (q_gain * c)[:, None] * cos.T
    qsin = (_swap_quarters(q_gain, 0) * sign * c)[:, None] * sin.T
    wqv_t = jnp.concatenate([w_qkv[:, :D_Q], w_qkv[:, D_Q + D_KV:]], axis=1).T.astype(BF16)
    wk = w_qkv[:, D_Q:D_Q + D_KV].astype(BF16)
    tiles_per_seq = s // TOKEN_ROWS
    row = lambda width: pl.BlockSpec((TOKEN_ROWS, width), lambda i: (i, 0))
    pos = pl.BlockSpec((TOKEN_ROWS, HEAD_DIM), lambda i: (i % tiles_per_seq, 0))
    pos_t = pl.BlockSpec((HEAD_DIM, TOKEN_ROWS), lambda i: (0, i % tiles_per_seq))
    qt, k, vt = pl.pallas_call(
        _qkv_body,
        out_shape=(jax.ShapeDtypeStruct((D_Q, n), BF16),
                   jax.ShapeDtypeStruct((n, D_KV), BF16),
                   jax.ShapeDtypeStruct((N_KV_HEADS, V_AUG_ROWS, n), BF16)),
        grid=(n // TOKEN_ROWS,),
        in_specs=[row(D_MODEL), _resident((1, D_MODEL)),
                  _resident((D_Q + D_KV, D_MODEL)), _resident((D_MODEL, D_KV)),
                  _resident((1, HEAD_DIM)), pos_t, pos_t, pos, pos, pos],
        out_specs=(pl.BlockSpec((D_Q, TOKEN_ROWS), lambda i: (0, i)), row(D_KV),
                   pl.BlockSpec((N_KV_HEADS, V_AUG_ROWS, TOKEN_ROWS), lambda i: (0, 0, i))),
        compiler_params=_params("parallel"),
        name="attn_qkv",
    )(x2d, gain.reshape(1, D_MODEL), wqv_t, wk, k_gain.reshape(1, HEAD_DIM),
      qcos, qsin, cos, ksup, ksdn)

    group_w = Q_PER_KV * HEAD_DIM
    attn = pl.pallas_call(
        functools.partial(_attn_body, seq=s),
        out_shape=jax.ShapeDtypeStruct((b, s, D_Q), BF16),
        grid=(b, N_KV_HEADS),
        in_specs=[pl.BlockSpec((group_w, s), lambda bi, g: (g, bi)),
                  pl.BlockSpec((s, HEAD_DIM), lambda bi, g: (bi, g)),
                  pl.BlockSpec((1, V_AUG_ROWS, s), lambda bi, g: (g, 0, bi))],
        out_specs=pl.BlockSpec((1, s, group_w), lambda bi, g: (bi, 0, g)),
        scratch_shapes=[pltpu.VMEM((2, ATTN_KV_ROWS, Q_PER_KV * ATTN_Q_ROWS), F32)],
        compiler_params=_params("parallel", "parallel"),
        name="attn_core",
    )(qt, k, vt)
    return attn.reshape(n, D_Q), w_o.astype(BF16)


def kernel(x, ffn1_norm, ffn1_w_gate, ffn1_w_up, ffn1_w_down, mixer_norm, ffn2_norm, ffn2_w_gate, ffn2_w_up, ffn2_w_down, pool_w, pool_b, pool_scale, fourier_w, fourier_b, attn_w_qkv, attn_q_norm, attn_k_norm, attn_w_o, final_norm):
    b, s, d = x.shape
    depth = ffn1_norm.shape[0]
    n = b * s
    ffn1 = (ffn1_norm.reshape(depth, 1, d), ffn1_w_gate.astype(BF16),
            ffn1_w_up.astype(BF16), ffn1_w_down.astype(BF16))
    ffn2 = (ffn2_norm.reshape(depth, 1, d), ffn2_w_gate.astype(BF16),
            ffn2_w_up.astype(BF16), ffn2_w_down.astype(BF16))

    def ffn(x3d, layer, params, final_gain=None, proj=None):
        out = _ffn(x3d.reshape(n, d), layer, *params, final_gain=final_gain, proj=proj)
        return out.reshape(b, s, d)

    for i in range(depth):
        x = ffn(x, i, ffn1)
        kind, j = i % N_MIXERS, i // N_MIXERS
        proj = None
        if kind == 0:
            x = _pool_mixer(x, mixer_norm[i], pool_w[j], pool_b[j], pool_scale[j])
        elif kind == 1:
            x = _fourier_mixer(x, mixer_norm[i], fourier_w[j], fourier_b[j])
        else:
            proj = _attention_mixer(x, mixer_norm[i], attn_w_qkv[j], attn_q_norm[j],
                                    attn_k_norm[j], attn_w_o[j])
        x = ffn(x, i, ffn2, final_norm if i == depth - 1 else None, proj)
    return x
```

```python
import functools
import math

import jax
import jax.numpy as jnp
from jax import lax
from jax.experimental import pallas as pl
from jax.experimental.pallas import tpu as pltpu

D_MODEL = 1024
D_FF = 2816
N_MIXERS = 3
NORM_EPS = 1e-6
POOL_WINDOWS = (2, 4, 8, 16)
POOL_GROUP = D_MODEL // len(POOL_WINDOWS)
POOL_HALO = max(POOL_WINDOWS) // 2
N_FOURIER_GROUPS = 4
FOURIER_GROUP = D_MODEL // N_FOURIER_GROUPS
DFT_RADIX = 4
HEAD_DIM = 128
N_Q_HEADS = D_MODEL // HEAD_DIM
N_KV_HEADS = N_Q_HEADS // 4
Q_PER_KV = N_Q_HEADS // N_KV_HEADS
D_Q = N_Q_HEADS * HEAD_DIM
D_KV = N_KV_HEADS * HEAD_DIM
V_AUG_ROWS = HEAD_DIM + 16
GRID_W = 64
ROPE_THETA = 10000.0

V7X_MXU_DIM = 256
V7X_BF16_ROWS = 16
V7X_VMEM_BYTES = 64 * 1024 * 1024
VMEM_LIMIT_BYTES = V7X_VMEM_BYTES * 7 // 8

FFN_ROWS = 1024
FFN_COLS = V7X_MXU_DIM
POOL_ROWS = 1024
POOL_SUB_ROWS = V7X_MXU_DIM
POOL_PAD = V7X_BF16_ROWS
TOKEN_ROWS = 1024
CHAN_DFT_ROWS = 256
SEQ_DFT_ROWS = V7X_MXU_DIM
SEQ_DFT_GROUP = V7X_MXU_DIM // DFT_RADIX
ATTN_Q_ROWS = 512
ATTN_KV_ROWS = 512
ATTN_COL_GROUPS = 8

F32 = jnp.float32
BF16 = jnp.bfloat16


def _params(*semantics):
    return pltpu.CompilerParams(dimension_semantics=semantics,
                                vmem_limit_bytes=VMEM_LIMIT_BYTES)


def _resident(shape, index=None):
    lead = tuple(index if d is None else 0 for d in shape)
    return pl.BlockSpec(shape, lambda *_: lead, pipeline_mode=pl.Buffered(1))


def _rms(x, gain):
    ms = jnp.mean(x * x, axis=-1, keepdims=True)
    return x * lax.rsqrt(ms + NORM_EPS) * gain


def _dot(a, b):
    return jnp.dot(a, b, preferred_element_type=F32)


def _dot_bt(a, b):
    return lax.dot_general(a, b, (((1,), (1,)), ((), ())), preferred_element_type=F32)


def _ffn_body(*refs, final_norm, pre_proj):
    refs = list(refs)
    x_ref = refs.pop(0)
    x = x_ref[...]
    if pre_proj:
        a_ref, wp_ref = refs.pop(0), refs.pop(0)
        x = x + _dot(a_ref[...], wp_ref[...])
    g_ref, wg_ref, wu_ref, wd_ref = refs[:4]
    o_ref, act_ref = refs[-2:]
    h = _rms(x, g_ref[...]).astype(BF16)
    for c in range(D_FF // FFN_COLS):
        cols = slice(c * FFN_COLS, (c + 1) * FFN_COLS)
        gate = _dot(h, wg_ref[:, cols])
        up = _dot(h, wu_ref[:, cols])
        act_ref[:, cols] = (gate * jax.nn.sigmoid(gate) * up).astype(BF16)
    out = x + 0.5 * _dot(act_ref[...], wd_ref[...])
    if final_norm:
        out = _rms(out, refs[4][...])
    o_ref[...] = out


def _ffn(x2d, layer, gains, wg, wu, wd, final_gain=None, proj=None):
    n = x2d.shape[0]
    row_spec = pl.BlockSpec((FFN_ROWS, D_MODEL), lambda i: (i, 0))
    in_specs, args = [row_spec], [x2d]
    if proj is not None:
        a, w_proj = proj
        in_specs += [pl.BlockSpec((FFN_ROWS, a.shape[1]), lambda i: (i, 0)),
                     _resident(w_proj.shape)]
        args += [a, w_proj]
    in_specs += [_resident((None, 1, D_MODEL), layer),
                 _resident((None, D_MODEL, D_FF), layer),
                 _resident((None, D_MODEL, D_FF), layer),
                 _resident((None, D_FF, D_MODEL), layer)]
    args += [gains, wg, wu, wd]
    if final_gain is not None:
        in_specs.append(_resident((1, D_MODEL)))
        args.append(final_gain.reshape(1, D_MODEL))
    name = "ffn" + ("_proj" if proj is not None else "") + ("_final" if final_gain is not None else "")
    return pl.pallas_call(
        functools.partial(_ffn_body, final_norm=final_gain is not None,
                          pre_proj=proj is not None),
        out_shape=jax.ShapeDtypeStruct(x2d.shape, F32),
        grid=(n // FFN_ROWS,),
        in_specs=in_specs,
        out_specs=row_spec,
        scratch_shapes=[pltpu.VMEM((FFN_ROWS, D_FF), BF16)],
        compiler_params=_params("parallel"),
        name=name,
    )(*args)


def _pool_bands():
    t = jnp.arange(POOL_SUB_ROWS)[:, None]
    r = jnp.arange(POOL_SUB_ROWS + 2 * POOL_PAD)[None, :] - POOL_PAD
    bands = [((r >= t - w // 2) & (r < t + w // 2)) for w in POOL_WINDOWS]
    return jnp.stack(bands).astype(BF16)


def _pool_body(xm_ref, xp_ref, xn_ref, g_ref, band_ref, pw_ref, pb_ref, ps_ref,
               o_ref, ext_ref, *, seq):
    i = pl.program_id(1)
    gain = g_ref[...]
    fill = jnp.zeros((POOL_PAD - POOL_HALO, D_MODEL), F32)
    hp = jnp.where(i > 0, _rms(xp_ref[0], gain), 0.0)
    hn = jnp.where(i < pl.num_programs(1) - 1, _rms(xn_ref[0], gain), 0.0)
    ext_ref[0:POOL_PAD] = jnp.concatenate([fill, hp], axis=0)
    ext_ref[POOL_PAD:POOL_PAD + POOL_ROWS] = _rms(xm_ref[0], gain)
    ext_ref[POOL_PAD + POOL_ROWS:] = jnp.concatenate([hn, fill], axis=0)
    win_rows = POOL_SUB_ROWS + 2 * POOL_PAD
    sub_starts = range(0, POOL_ROWS, POOL_SUB_ROWS)
    for g, w in enumerate(POOL_WINDOWS):
        cols = slice(g * POOL_GROUP, (g + 1) * POOL_GROUP)
        terms = []
        for r0 in sub_starts:
            win = ext_ref[r0:r0 + win_rows, cols]
            win_hi = win.astype(BF16)
            terms += [win_hi, (win - win_hi.astype(F32)).astype(BF16)]
        wsums = _dot(band_ref[g], jnp.concatenate(terms, axis=1))
        pooled = []
        for k, r0 in enumerate(sub_starts):
            hi_cols = slice(2 * k * POOL_GROUP, (2 * k + 1) * POOL_GROUP)
            lo_cols = slice((2 * k + 1) * POOL_GROUP, (2 * k + 2) * POOL_GROUP)
            t = i * POOL_ROWS + r0 + lax.broadcasted_iota(jnp.int32, (POOL_SUB_ROWS, 1), 0)
            cnt = jnp.minimum(t + w // 2, seq) - jnp.maximum(t - w // 2, 0)
            hm = ext_ref[POOL_PAD + r0:POOL_PAD + r0 + POOL_SUB_ROWS, cols]
            pooled.append((wsums[:, hi_cols] + wsums[:, lo_cols]) / cnt.astype(F32) - hm)
        y = _dot(jnp.concatenate(pooled, axis=0).astype(BF16), pw_ref[g]) + pb_ref[g]
        o_ref[0, :, cols] = xm_ref[0, :, cols] + y * ps_ref[:, cols]


def _pool_mixer(x, gain, pw, pb, ps):
    b, s, _ = x.shape
    blocks_per_tile = POOL_ROWS // POOL_HALO
    n_halo_blocks = s // POOL_HALO
    n_groups = len(POOL_WINDOWS)
    return pl.pallas_call(
        functools.partial(_pool_body, seq=s),
        out_shape=jax.ShapeDtypeStruct(x.shape, F32),
        grid=(b, s // POOL_ROWS),
        in_specs=[
            pl.BlockSpec((1, POOL_ROWS, D_MODEL), lambda bi, i: (bi, i, 0)),
            pl.BlockSpec((1, POOL_HALO, D_MODEL),
                         lambda bi, i: (bi, jnp.maximum(i * blocks_per_tile - 1, 0), 0)),
            pl.BlockSpec((1, POOL_HALO, D_MODEL),
                         lambda bi, i: (bi, jnp.minimum((i + 1) * blocks_per_tile,
                                                        n_halo_blocks - 1), 0)),
            _resident((1, D_MODEL)),
            _resident((n_groups, POOL_SUB_ROWS, POOL_SUB_ROWS + 2 * POOL_PAD)),
            _resident((n_groups, POOL_GROUP, POOL_GROUP)),
            _resident((n_groups, 1, POOL_GROUP)),
            _resident((1, D_MODEL)),
        ],
        out_specs=pl.BlockSpec((1, POOL_ROWS, D_MODEL), lambda bi, i: (bi, i, 0)),
        scratch_shapes=[pltpu.VMEM((POOL_ROWS + 2 * POOL_PAD, D_MODEL), F32)],
        compiler_params=_params("parallel", "parallel"),
        name="pool_mixer",
    )(x, x, x, gain.reshape(1, D_MODEL), _pool_bands(), pw.astype(BF16),
      pb.reshape(n_groups, 1, POOL_GROUP), ps.reshape(1, D_MODEL))


def _dft_tables(n_rows, n_cols, n):
    k = (jnp.arange(n_rows, dtype=jnp.int32)[:, None]
         * jnp.arange(n_cols, dtype=jnp.int32)[None, :]) % n
    ang = k.astype(F32) * (2.0 * math.pi / n)
    return jnp.cos(ang), jnp.sin(ang)


def _seq_dft_matrix(s):
    quarter = s // DFT_RADIX
    root = int(math.isqrt(quarter))
    assert root * root == quarter
    j = jnp.arange(DFT_RADIX * 2 * quarter, dtype=jnp.int32)
    r, part, t = j // (2 * quarter), (j // quarter) % 2, j % quarter
    beta = ((t * r) % s).astype(F32) * (2.0 * math.pi / s) - part.astype(F32) * (math.pi / 2)
    rows = jnp.arange(root, dtype=jnp.int32)[:, None]
    ang_a = ((rows * t[None, :]) % root).astype(F32) * (2.0 * math.pi / root)
    ang_b = ((rows * t[None, :]) % quarter).astype(F32) * (2.0 * math.pi / quarter) + beta[None, :]
    w = (jnp.cos(ang_a)[:, None, :] * jnp.cos(ang_b)[None, :, :]
         - jnp.sin(ang_a)[:, None, :] * jnp.sin(ang_b)[None, :, :])
    return w.astype(BF16).reshape(quarter, DFT_RADIX * 2 * quarter)


def _chan_dft_body(x_ref, g_ref, cc_ref, nsc_ref, u_ref):
    gain = g_ref[...]
    z_re, z_im = [], []
    for q in range(DFT_RADIX):
        h = _rms(x_ref[0, q], gain).astype(BF16)
        re, im = [], []
        for g in range(N_FOURIER_GROUPS):
            cols = slice(g * FOURIER_GROUP, (g + 1) * FOURIER_GROUP)
            re.append(_dot(h[:, cols], cc_ref[...]))
            im.append(_dot(h[:, cols], nsc_ref[...]))
        z_re.append(jnp.concatenate(re, axis=-1))
        z_im.append(jnp.concatenate(im, axis=-1))
    t0 = (z_re[0] + z_re[2], z_im[0] + z_im[2])
    t1 = (z_re[0] - z_re[2], z_im[0] - z_im[2])
    t2 = (z_re[1] + z_re[3], z_im[1] + z_im[3])
    t3 = (z_re[1] - z_re[3], z_im[1] - z_im[3])
    u = [(t0[0] + t2[0], t0[1] + t2[1]),
         (t1[0] + t3[1], t1[1] - t3[0]),
         (t0[0] - t2[0], t0[1] - t2[1]),
         (t1[0] - t3[1], t1[1] + t3[0])]
    for r in range(DFT_RADIX):
        u_ref[0, r, 0] = u[r][0].astype(BF16)
        u_ref[0, r, 1] = u[r][1].astype(BF16)


def _interleave_matrix():
    out_row = jnp.arange(DFT_RADIX * SEQ_DFT_GROUP)[:, None]
    in_row = jnp.arange(DFT_RADIX * SEQ_DFT_GROUP)[None, :]
    src = (out_row % DFT_RADIX) * SEQ_DFT_GROUP + out_row // DFT_RADIX
    return (in_row == src).astype(BF16)


def _seq_dft_body(w_ref, u_ref, x_ref, wo_ref, bo_ref, perm_ref, o_ref, *, inv_norm):
    rows, two_q = w_ref.shape[0], u_ref.shape[2]
    f = [(_dot(w_ref[:, r * two_q:(r + 1) * two_q], u_ref[0, r]) * inv_norm).astype(BF16)
         for r in range(DFT_RADIX)]
    span = DFT_RADIX * SEQ_DFT_GROUP
    for j in range(rows // SEQ_DFT_GROUP):
        grp = slice(j * SEQ_DFT_GROUP, (j + 1) * SEQ_DFT_GROUP)
        stacked = jnp.concatenate([f[r][grp] for r in range(DFT_RADIX)], axis=0)
        f_tok = _dot(perm_ref[...], stacked).astype(BF16)
        tok = slice(j * span, (j + 1) * span)
        o_ref[0, tok, :] = x_ref[0, tok, :] + _dot(f_tok, wo_ref[...]) + bo_ref[...]


def _fourier_mixer(x, gain, w_out, b_out):
    b, s, _ = x.shape
    quarter = s // DFT_RADIX
    cc, sc = _dft_tables(FOURIER_GROUP, FOURIER_GROUP, FOURIER_GROUP)
    u = pl.pallas_call(
        _chan_dft_body,
        out_shape=jax.ShapeDtypeStruct((b, DFT_RADIX, 2, quarter, D_MODEL), BF16),
        grid=(b, quarter // CHAN_DFT_ROWS),
        in_specs=[pl.BlockSpec((1, DFT_RADIX, CHAN_DFT_ROWS, D_MODEL),
                               lambda bi, i: (bi, 0, i, 0)),
                  _resident((1, D_MODEL)),
                  _resident((FOURIER_GROUP, FOURIER_GROUP)),
                  _resident((FOURIER_GROUP, FOURIER_GROUP))],
        out_specs=pl.BlockSpec((1, DFT_RADIX, 2, CHAN_DFT_ROWS, D_MODEL),
                               lambda bi, i: (bi, 0, 0, i, 0)),
        compiler_params=_params("parallel", "parallel"),
        name="fourier_chan_dft",
    )(x.reshape(b, DFT_RADIX, quarter, D_MODEL), gain.reshape(1, D_MODEL),
      cc.astype(BF16), (-sc).astype(BF16))
    u = u.reshape(b, DFT_RADIX, 2 * quarter, D_MODEL)
    inv_norm = 1.0 / math.sqrt(s * FOURIER_GROUP)
    tile = pl.BlockSpec((1, DFT_RADIX * SEQ_DFT_ROWS, D_MODEL), lambda bi, i: (bi, i, 0))
    return pl.pallas_call(
        functools.partial(_seq_dft_body, inv_norm=inv_norm),
        out_shape=jax.ShapeDtypeStruct(x.shape, F32),
        grid=(b, quarter // SEQ_DFT_ROWS),
        in_specs=[pl.BlockSpec((SEQ_DFT_ROWS, DFT_RADIX * 2 * quarter), lambda bi, i: (i, 0)),
                  pl.BlockSpec((1, DFT_RADIX, 2 * quarter, D_MODEL),
                               lambda bi, i: (bi, 0, 0, 0), pipeline_mode=pl.Buffered(1)),
                  tile,
                  _resident((D_MODEL, D_MODEL)),
                  _resident((1, D_MODEL)),
                  _resident((DFT_RADIX * SEQ_DFT_GROUP, DFT_RADIX * SEQ_DFT_GROUP))],
        out_specs=tile,
        compiler_params=_params("parallel", "arbitrary"),
        name="fourier_seq_dft",
    )(_seq_dft_matrix(s), u, x, w_out.astype(BF16), b_out.reshape(1, D_MODEL),
      _interleave_matrix())


def _rope_angles(s):
    rows = s // GRID_W
    row = jnp.repeat(jnp.arange(rows, dtype=F32), GRID_W)
    col = jnp.tile(jnp.arange(GRID_W, dtype=F32), rows)
    half = HEAD_DIM // 2
    inv_freq = ROPE_THETA ** (-jnp.arange(0, half, 2, dtype=F32) / half)
    ang_r = row[:, None] * inv_freq[None, :]
    ang_c = col[:, None] * inv_freq[None, :]
    return jnp.concatenate([ang_r, ang_r, ang_c, ang_c], axis=-1)


def _swap_quarters(a, axis):
    q0, q1, q2, q3 = jnp.split(a, 4, axis=axis)
    return jnp.concatenate([q1, q0, q3, q2], axis=axis)


def _qkv_body(x_ref, g_ref, wqv_ref, wk_ref, kg_ref, qcos_ref, qsin_ref,
              kcos_ref, ksup_ref, ksdn_ref, qt_ref, k_ref, vt_ref):
    h = _rms(x_ref[...], g_ref[...]).astype(BF16)
    rows = h.shape[0]
    k2 = _dot(h, wk_ref[...])
    kcos, ksup, ksdn = kcos_ref[...], ksup_ref[...], ksdn_ref[...]
    quarter = HEAD_DIM // 4
    for g in range(N_KV_HEADS):
        u = k2[:, g * HEAD_DIM:(g + 1) * HEAD_DIM]
        ms = jnp.mean(u * u, axis=-1, keepdims=True)
        u = u * lax.rsqrt(ms + NORM_EPS) * kg_ref[...]
        up = pltpu.roll(u, HEAD_DIM - quarter, 1)
        dn = pltpu.roll(u, quarter, 1)
        k_ref[:, g * HEAD_DIM:(g + 1) * HEAD_DIM] = (
            u * kcos + (up * ksup + dn * ksdn)).astype(BF16)
    ut = _dot_bt(wqv_ref[...], h)
    qcos, qsin = qcos_ref[...], qsin_ref[...]
    for hd in range(N_Q_HEADS):
        u = ut[hd * HEAD_DIM:(hd + 1) * HEAD_DIM, :]
        ms = jnp.mean(u * u, axis=0, keepdims=True)
        n = u * lax.rsqrt(ms + NORM_EPS)
        r = n * qcos + _swap_quarters(n, 0) * qsin
        qt_ref[hd * HEAD_DIM:(hd + 1) * HEAD_DIM, :] = r.astype(BF16)
    ones_row = lax.broadcasted_iota(jnp.int32, (V_AUG_ROWS - HEAD_DIM, rows), 0) == 0
    for g in range(N_KV_HEADS):
        v_t = ut[D_Q + g * HEAD_DIM:D_Q + (g + 1) * HEAD_DIM, :]
        vt_ref[g, 0:HEAD_DIM, :] = v_t.astype(BF16)
        vt_ref[g, HEAD_DIM:, :] = jnp.where(ones_row, 1.0, 0.0).astype(BF16)


def _attn_body(qt_ref, k_ref, vt_ref, o_ref, s_ref, *, seq):
    n_chunks = seq // ATTN_KV_ROWS
    groups = range(ATTN_COL_GROUPS)
    width = Q_PER_KV * ATTN_Q_ROWS // ATTN_COL_GROUPS

    def scores(c, g, q):
        st = _dot(k_ref[c * ATTN_KV_ROWS:(c + 1) * ATTN_KV_ROWS, :], q)
        s_ref[c % 2, :, g * width:(g + 1) * width] = st
        return jnp.max(st, axis=0, keepdims=True)

    def q_block(i, carry):
        r0 = pl.multiple_of(i * ATTN_Q_ROWS, ATTN_Q_ROWS)
        qb = qt_ref[:, pl.ds(r0, ATTN_Q_ROWS)]
        qcat = jnp.concatenate([qb[r * HEAD_DIM:(r + 1) * HEAD_DIM, :]
                                for r in range(Q_PER_KV)], axis=1)
        q = [qcat[:, g * width:(g + 1) * width] for g in groups]
        m = [None] * ATTN_COL_GROUPS
        acc = [None] * ATTN_COL_GROUPS
        cm = [scores(0, g, q[g]) for g in groups]
        for c in range(n_chunks):
            kv = slice(c * ATTN_KV_ROWS, (c + 1) * ATTN_KV_ROWS)
            for g in groups:
                cm_next = scores(c + 1, g, q[g]) if c + 1 < n_chunks else None
                m_new = cm[g] if m[g] is None else jnp.maximum(m[g], cm[g])
                p = jnp.exp2(s_ref[c % 2, :, g * width:(g + 1) * width] - m_new).astype(BF16)
                pv = _dot(vt_ref[0, :, kv], p)
                acc[g] = pv if m[g] is None else acc[g] * jnp.exp2(m[g] - m_new) + pv
                m[g], cm[g] = m_new, cm_next
        out_t = jnp.concatenate([a[0:HEAD_DIM] / a[HEAD_DIM:HEAD_DIM + 1] for a in acc], axis=1)
        for r in range(Q_PER_KV):
            head = out_t[:, r * ATTN_Q_ROWS:(r + 1) * ATTN_Q_ROWS].T
            o_ref[0, pl.ds(r0, ATTN_Q_ROWS), r * HEAD_DIM:(r + 1) * HEAD_DIM] = head.astype(BF16)
        return carry

    lax.fori_loop(0, seq // ATTN_Q_ROWS, q_block, 0)


def _attention_mixer(x, gain, w_qkv, q_gain, k_gain, w_o):
    b, s, _ = x.shape
    n = b * s
    x2d = x.reshape(n, D_MODEL)
    ang = _rope_angles(s)
    cos, sin = jnp.cos(ang), jnp.sin(ang)
    first = ((jnp.arange(HEAD_DIM) % (HEAD_DIM // 2)) < HEAD_DIM // 4)
    ksup = jnp.where(first[None, :], -sin, 0.0)
    ksdn = jnp.where(first[None, :], 0.0, sin)
    c = HEAD_DIM ** -0.5 * math.log2(math.e)
    sign = jnp.where(first, -1.0, 1.0)
    qcos = (q_gain * c)[:, None] * cos.T
    qsin = (_swap_quarters(q_gain, 0) * sign * c)[:, None] * sin.T
    wqv_t = jnp.concatenate([w_qkv[:, :D_Q], w_qkv[:, D_Q + D_KV:]], axis=1).T.astype(BF16)
    wk = w_qkv[:, D_Q:D_Q + D_KV].astype(BF16)
    tiles_per_seq = s // TOKEN_ROWS
    row = lambda width: pl.BlockSpec((TOKEN_ROWS, width), lambda i: (i, 0))
    pos = pl.BlockSpec((TOKEN_ROWS, HEAD_DIM), lambda i: (i % tiles_per_seq, 0))
    pos_t = pl.BlockSpec((HEAD_DIM, TOKEN_ROWS), lambda i: (0, i % tiles_per_seq))
    qt, k, vt = pl.pallas_call(
        _qkv_body,
        out_shape=(jax.ShapeDtypeStruct((D_Q, n), BF16),
                   jax.ShapeDtypeStruct((n, D_KV), BF16),
                   jax.ShapeDtypeStruct((N_KV_HEADS, V_AUG_ROWS, n), BF16)),
        grid=(n // TOKEN_ROWS,),
        in_specs=[row(D_MODEL), _resident((1, D_MODEL)),
                  _resident((D_Q + D_KV, D_MODEL)), _resident((D_MODEL, D_KV)),
                  _resident((1, HEAD_DIM)), pos_t, pos_t, pos, pos, pos],
        out_specs=(pl.BlockSpec((D_Q, TOKEN_ROWS), lambda i: (0, i)), row(D_KV),
                   pl.BlockSpec((N_KV_HEADS, V_AUG_ROWS, TOKEN_ROWS), lambda i: (0, 0, i))),
        compiler_params=_params("parallel"),
        name="attn_qkv",
    )(x2d, gain.reshape(1, D_MODEL), wqv_t, wk, k_gain.reshape(1, HEAD_DIM),
      qcos, qsin, cos, ksup, ksdn)

    group_w = Q_PER_KV * HEAD_DIM
    attn = pl.pallas_call(
        functools.partial(_attn_body, seq=s),
        out_shape=jax.ShapeDtypeStruct((b, s, D_Q), BF16),
        grid=(b, N_KV_HEADS),
        in_specs=[pl.BlockSpec((group_w, s), lambda bi, g: (g, bi)),
                  pl.BlockSpec((s, HEAD_DIM), lambda bi, g: (bi, g)),
                  pl.BlockSpec((1, V_AUG_ROWS, s), lambda bi, g: (g, 0, bi))],
        out_specs=pl.BlockSpec((1, s, group_w), lambda bi, g: (bi, 0, g)),
        scratch_shapes=[pltpu.VMEM((2, ATTN_KV_ROWS, Q_PER_KV * ATTN_Q_ROWS), F32)],
        compiler_params=_params("parallel", "parallel"),
        name="attn_core",
    )(qt, k, vt)
    return attn.reshape(n, D_Q), w_o.astype(BF16)


def kernel(x, ffn1_norm, ffn1_w_gate, ffn1_w_up, ffn1_w_down, mixer_norm, ffn2_norm, ffn2_w_gate, ffn2_w_up, ffn2_w_down, pool_w, pool_b, pool_scale, fourier_w, fourier_b, attn_w_qkv, attn_q_norm, attn_k_norm, attn_w_o, final_norm):
    b, s, d = x.shape
    depth = ffn1_norm.shape[0]
    n = b * s
    ffn1 = (ffn1_norm.reshape(depth, 1, d), ffn1_w_gate.astype(BF16),
            ffn1_w_up.astype(BF16), ffn1_w_down.astype(BF16))
    ffn2 = (ffn2_norm.reshape(depth, 1, d), ffn2_w_gate.astype(BF16),
            ffn2_w_up.astype(BF16), ffn2_w_down.astype(BF16))

    def ffn(x3d, layer, params, final_gain=None, proj=None):
        out = _ffn(x3d.reshape(n, d), layer, *params, final_gain=final_gain, proj=proj)
        return out.reshape(b, s, d)

    for i in range(depth):
        x = ffn(x, i, ffn1)
        kind, j = i % N_MIXERS, i // N_MIXERS
        proj = None
        if kind == 0:
            x = _pool_mixer(x, mixer_norm[i], pool_w[j], pool_b[j], pool_scale[j])
        elif kind == 1:
            x = _fourier_mixer(x, mixer_norm[i], fourier_w[j], fourier_b[j])
        else:
            proj = _attention_mixer(x, mixer_norm[i], attn_w_qkv[j], attn_q_norm[j],
                                    attn_k_norm[j], attn_w_o[j])
        x = ffn(x, i, ffn2, final_norm if i == depth - 1 else None, proj)
    return x
```

```python
import functools
import math

import jax
import jax.numpy as jnp
from jax import lax
from jax.experimental import pallas as pl
from jax.experimental.pallas import tpu as pltpu

D_MODEL = 1024
D_FF = 2816
N_MIXERS = 3
NORM_EPS = 1e-6
POOL_WINDOWS = (2, 4, 8, 16)
POOL_GROUP = D_MODEL // len(POOL_WINDOWS)
POOL_HALO = max(POOL_WINDOWS) // 2
N_FOURIER_GROUPS = 4
FOURIER_GROUP = D_MODEL // N_FOURIER_GROUPS
DFT_RADIX = 4
HEAD_DIM = 128
N_Q_HEADS = D_MODEL // HEAD_DIM
N_KV_HEADS = N_Q_HEADS // 4
Q_PER_KV = N_Q_HEADS // N_KV_HEADS
D_Q = N_Q_HEADS * HEAD_DIM
D_KV = N_KV_HEADS * HEAD_DIM
V_AUG_ROWS = HEAD_DIM + 16
GRID_W = 64
ROPE_THETA = 10000.0

V7X_MXU_DIM = 256
V7X_BF16_ROWS = 16
V7X_VMEM_BYTES = 64 * 1024 * 1024
VMEM_LIMIT_BYTES = V7X_VMEM_BYTES * 7 // 8

FFN_ROWS = 1024
FFN_COLS = V7X_MXU_DIM
POOL_ROWS = 512
POOL_PAD = V7X_BF16_ROWS
TOKEN_ROWS = 512
CHAN_DFT_ROWS = 256
SEQ_DFT_ROWS = V7X_MXU_DIM
SEQ_DFT_GROUP = V7X_MXU_DIM // DFT_RADIX
ATTN_Q_ROWS = 512
ATTN_KV_ROWS = 512
ATTN_COL_GROUPS = 8

F32 = jnp.float32
BF16 = jnp.bfloat16


def _params(*semantics):
    return pltpu.CompilerParams(dimension_semantics=semantics,
                                vmem_limit_bytes=VMEM_LIMIT_BYTES)


def _resident(shape, index=None):
    lead = tuple(index if d is None else 0 for d in shape)
    return pl.BlockSpec(shape, lambda *_: lead, pipeline_mode=pl.Buffered(1))


def _rms(x, gain):
    ms = jnp.mean(x * x, axis=-1, keepdims=True)
    return x * lax.rsqrt(ms + NORM_EPS) * gain


def _dot(a, b):
    return jnp.dot(a, b, preferred_element_type=F32)


def _dot_bt(a, b):
    return lax.dot_general(a, b, (((1,), (1,)), ((), ())), preferred_element_type=F32)


def _swiglu_into(h, wg_ref, wu_ref, act_ref, side_work=()):
    n_chunks = D_FF // FFN_COLS
    after = {((k + 1) * n_chunks) // (len(side_work) + 1) - 1: piece
             for k, piece in enumerate(side_work)}
    for c in range(n_chunks):
        cols = slice(c * FFN_COLS, (c + 1) * FFN_COLS)
        gate = _dot(h[...], wg_ref[:, cols])
        up = _dot(h[...], wu_ref[:, cols])
        act_ref[:, cols] = (gate * jax.nn.sigmoid(gate) * up).astype(BF16)
        if c in after:
            after[c]()


def _ffn_body(*refs, final_norm, pre_proj):
    refs = list(refs)
    x_ref = refs.pop(0)
    x = x_ref[...]
    if pre_proj:
        a_ref, wp_ref = refs.pop(0), refs.pop(0)
        x = x + _dot(a_ref[...], wp_ref[...])
    g_ref, wg_ref, wu_ref, wd_ref = refs[:4]
    o_ref, act_ref = refs[-2:]
    _swiglu_into(_rms(x, g_ref[...]).astype(BF16), wg_ref, wu_ref, act_ref)
    out = x + 0.5 * _dot(act_ref[...], wd_ref[...])
    if final_norm:
        out = _rms(out, refs[4][...])
    o_ref[...] = out


def _ffn(x2d, layer, gains, wg, wu, wd, final_gain=None, proj=None):
    n = x2d.shape[0]
    row_spec = pl.BlockSpec((FFN_ROWS, D_MODEL), lambda i: (i, 0))
    in_specs, args = [row_spec], [x2d]
    if proj is not None:
        a, w_proj = proj
        in_specs += [pl.BlockSpec((FFN_ROWS, a.shape[1]), lambda i: (i, 0)),
                     _resident(w_proj.shape)]
        args += [a, w_proj]
    in_specs += [_resident((None, 1, D_MODEL), layer),
                 _resident((None, D_MODEL, D_FF), layer),
                 _resident((None, D_MODEL, D_FF), layer),
                 _resident((None, D_FF, D_MODEL), layer)]
    args += [gains, wg, wu, wd]
    if final_gain is not None:
        in_specs.append(_resident((1, D_MODEL)))
        args.append(final_gain.reshape(1, D_MODEL))
    name = "ffn" + ("_proj" if proj is not None else "") + ("_final" if final_gain is not None else "")
    return pl.pallas_call(
        functools.partial(_ffn_body, final_norm=final_gain is not None,
                          pre_proj=proj is not None),
        out_shape=jax.ShapeDtypeStruct(x2d.shape, F32),
        grid=(n // FFN_ROWS,),
        in_specs=in_specs,
        out_specs=row_spec,
        scratch_shapes=[pltpu.VMEM((FFN_ROWS, D_FF), BF16)],
        compiler_params=_params("parallel"),
        name=name,
    )(*args)


def _pool_prepare(xm_ref, xp_ref, xn_ref, g_ref, ext_ref, *, i, n_i):
    gain = g_ref[...]
    fill = jnp.zeros((POOL_PAD - POOL_HALO, D_MODEL), F32)
    hp = jnp.where(i > 0, _rms(xp_ref[0], gain), 0.0)
    hn = jnp.where(i < n_i - 1, _rms(xn_ref[0], gain), 0.0)
    ext_ref[0:POOL_PAD] = jnp.concatenate([fill, hp], axis=0)
    ext_ref[POOL_PAD:POOL_PAD + POOL_ROWS] = _rms(xm_ref[0], gain)
    ext_ref[POOL_PAD + POOL_ROWS:] = jnp.concatenate([hn, fill], axis=0)


def _pool_group(g, xm_ref, pw_ref, pb_ref, ps_ref, o_ref, ext_ref, *, seq, i):
    w = POOL_WINDOWS[g]
    cols = slice(g * POOL_GROUP, (g + 1) * POOL_GROUP)
    wsum = None
    for k in range(-(w // 2), w // 2):
        shifted = ext_ref[POOL_PAD + k:POOL_PAD + k + POOL_ROWS, cols]
        wsum = shifted if wsum is None else wsum + shifted
    t = i * POOL_ROWS + lax.broadcasted_iota(jnp.int32, (POOL_ROWS, 1), 0)
    cnt = jnp.minimum(t + w // 2, seq) - jnp.maximum(t - w // 2, 0)
    hm = ext_ref[POOL_PAD:POOL_PAD + POOL_ROWS, cols]
    pooled = wsum / cnt.astype(F32) - hm
    y = _dot(pooled.astype(BF16), pw_ref[g]) + pb_ref[g]
    o_ref[:, cols] = xm_ref[0, :, cols] + y * ps_ref[:, cols]


def _pool_ffn_body(xm_ref, xp_ref, xn_ref, mg_ref, pw_ref, pb_ref, ps_ref,
                   g_ref, wg_ref, wu_ref, wd_ref, *rest, seq, final_norm):
    fg_ref = rest[0] if final_norm else None
    o_ref, ext_ref, mid_ref, h_ref, act_ref = rest[-5:]
    j = pl.program_id(0)
    n_tiles = pl.num_programs(0) - 1
    tiles_per_seq = seq // POOL_ROWS

    @pl.when(j == 0)
    def _():
        mid_ref[...] = jnp.zeros_like(mid_ref)

    x = mid_ref[...]
    o_ref[...] = x
    h_ref[...] = _rms(x, g_ref[...]).astype(BF16)
    i = jnp.minimum(j, n_tiles - 1) % tiles_per_seq
    pool_pieces = [functools.partial(_pool_prepare, xm_ref, xp_ref, xn_ref, mg_ref, ext_ref,
                                     i=i, n_i=tiles_per_seq)]
    pool_pieces += [functools.partial(_pool_group, g, xm_ref, pw_ref, pb_ref, ps_ref,
                                      mid_ref, ext_ref, seq=seq, i=i)
                    for g in range(len(POOL_WINDOWS))]
    _swiglu_into(h_ref, wg_ref, wu_ref, act_ref, side_work=pool_pieces)
    out = o_ref[...] + 0.5 * _dot(act_ref[...], wd_ref[...])
    if final_norm:
        out = _rms(out, fg_ref[...])
    o_ref[...] = out


def _pool_ffn(x, mixer_gain, pw, pb, ps, layer, gains, wg, wu, wd, final_gain=None):
    b, s, _ = x.shape
    tiles_per_seq = s // POOL_ROWS
    n_tiles = b * tiles_per_seq
    blocks_per_tile = POOL_ROWS // POOL_HALO
    n_halo_blocks = s // POOL_HALO
    n_groups = len(POOL_WINDOWS)

    def tile(j):
        jp = jnp.minimum(j, n_tiles - 1)
        return jp // tiles_per_seq, jp % tiles_per_seq

    def main_map(j):
        bi, i = tile(j)
        return bi, i, 0

    def prev_map(j):
        bi, i = tile(j)
        return bi, jnp.maximum(i * blocks_per_tile - 1, 0), 0

    def next_map(j):
        bi, i = tile(j)
        return bi, jnp.minimum((i + 1) * blocks_per_tile, n_halo_blocks - 1), 0

    in_specs = [
        pl.BlockSpec((1, POOL_ROWS, D_MODEL), main_map),
        pl.BlockSpec((1, POOL_HALO, D_MODEL), prev_map),
        pl.BlockSpec((1, POOL_HALO, D_MODEL), next_map),
        _resident((1, D_MODEL)),
        _resident((n_groups, POOL_GROUP, POOL_GROUP)),
        _resident((n_groups, 1, POOL_GROUP)),
        _resident((1, D_MODEL)),
        _resident((None, 1, D_MODEL), layer),
        _resident((None, D_MODEL, D_FF), layer),
        _resident((None, D_MODEL, D_FF), layer),
        _resident((None, D_FF, D_MODEL), layer),
    ]
    args = [x, x, x, mixer_gain.reshape(1, D_MODEL), pw.astype(BF16),
            pb.reshape(n_groups, 1, POOL_GROUP), ps.reshape(1, D_MODEL), gains, wg, wu, wd]
    if final_gain is not None:
        in_specs.append(_resident((1, D_MODEL)))
        args.append(final_gain.reshape(1, D_MODEL))
    out = pl.pallas_call(
        functools.partial(_pool_ffn_body, seq=s, final_norm=final_gain is not None),
        out_shape=jax.ShapeDtypeStruct((b * s, D_MODEL), F32),
        grid=(n_tiles + 1,),
        in_specs=in_specs,
        out_specs=pl.BlockSpec((POOL_ROWS, D_MODEL), lambda j: (jnp.maximum(j - 1, 0), 0)),
        scratch_shapes=[pltpu.VMEM((POOL_ROWS + 2 * POOL_PAD, D_MODEL), F32),
                        pltpu.VMEM((POOL_ROWS, D_MODEL), F32),
                        pltpu.VMEM((POOL_ROWS, D_MODEL), BF16),
                        pltpu.VMEM((POOL_ROWS, D_FF), BF16)],
        compiler_params=_params("arbitrary"),
        name="pool_ffn_final" if final_gain is not None else "pool_ffn",
    )(*args)
    return out.reshape(b, s, D_MODEL)


def _dft_tables(n_rows, n_cols, n):
    k = (jnp.arange(n_rows, dtype=jnp.int32)[:, None]
         * jnp.arange(n_cols, dtype=jnp.int32)[None, :]) % n
    ang = k.astype(F32) * (2.0 * math.pi / n)
    return jnp.cos(ang), jnp.sin(ang)


def _seq_dft_matrix(s):
    quarter = s // DFT_RADIX
    root = int(math.isqrt(quarter))
    assert root * root == quarter
    j = jnp.arange(DFT_RADIX * 2 * quarter, dtype=jnp.int32)
    r, part, t = j // (2 * quarter), (j // quarter) % 2, j % quarter
    beta = ((t * r) % s).astype(F32) * (2.0 * math.pi / s) - part.astype(F32) * (math.pi / 2)
    rows = jnp.arange(root, dtype=jnp.int32)[:, None]
    ang_a = ((rows * t[None, :]) % root).astype(F32) * (2.0 * math.pi / root)
    ang_b = ((rows * t[None, :]) % quarter).astype(F32) * (2.0 * math.pi / quarter) + beta[None, :]
    w = (jnp.cos(ang_a)[:, None, :] * jnp.cos(ang_b)[None, :, :]
         - jnp.sin(ang_a)[:, None, :] * jnp.sin(ang_b)[None, :, :])
    return w.astype(BF16).reshape(quarter, DFT_RADIX * 2 * quarter)


def _chan_dft_body(x_ref, g_ref, cc_ref, nsc_ref, u_ref):
    gain = g_ref[...]
    z_re, z_im = [], []
    for q in range(DFT_RADIX):
        h = _rms(x_ref[0, q], gain).astype(BF16)
        re, im = [], []
        for g in range(N_FOURIER_GROUPS):
            cols = slice(g * FOURIER_GROUP, (g + 1) * FOURIER_GROUP)
            re.append(_dot(h[:, cols], cc_ref[...]))
            im.append(_dot(h[:, cols], nsc_ref[...]))
        z_re.append(jnp.concatenate(re, axis=-1))
        z_im.append(jnp.concatenate(im, axis=-1))
    t0 = (z_re[0] + z_re[2], z_im[0] + z_im[2])
    t1 = (z_re[0] - z_re[2], z_im[0] - z_im[2])
    t2 = (z_re[1] + z_re[3], z_im[1] + z_im[3])
    t3 = (z_re[1] - z_re[3], z_im[1] - z_im[3])
    u = [(t0[0] + t2[0], t0[1] + t2[1]),
         (t1[0] + t3[1], t1[1] - t3[0]),
         (t0[0] - t2[0], t0[1] - t2[1]),
         (t1[0] - t3[1], t1[1] + t3[0])]
    for r in range(DFT_RADIX):
        u_ref[0, r, 0] = u[r][0].astype(BF16)
        u_ref[0, r, 1] = u[r][1].astype(BF16)


def _interleave_matrix():
    out_row = jnp.arange(DFT_RADIX * SEQ_DFT_GROUP)[:, None]
    in_row = jnp.arange(DFT_RADIX * SEQ_DFT_GROUP)[None, :]
    src = (out_row % DFT_RADIX) * SEQ_DFT_GROUP + out_row // DFT_RADIX
    return (in_row == src).astype(BF16)


def _seq_dft_body(w_ref, u_ref, x_ref, wo_ref, bo_ref, perm_ref, o_ref, *, inv_norm):
    rows, two_q = w_ref.shape[0], u_ref.shape[2]
    f = [(_dot(w_ref[:, r * two_q:(r + 1) * two_q], u_ref[0, r]) * inv_norm).astype(BF16)
         for r in range(DFT_RADIX)]
    span = DFT_RADIX * SEQ_DFT_GROUP
    for j in range(rows // SEQ_DFT_GROUP):
        grp = slice(j * SEQ_DFT_GROUP, (j + 1) * SEQ_DFT_GROUP)
        stacked = jnp.concatenate([f[r][grp] for r in range(DFT_RADIX)], axis=0)
        f_tok = _dot(perm_ref[...], stacked).astype(BF16)
        tok = slice(j * span, (j + 1) * span)
        o_ref[0, tok, :] = x_ref[0, tok, :] + _dot(f_tok, wo_ref[...]) + bo_ref[...]


def _fourier_mixer(x, gain, w_out, b_out):
    b, s, _ = x.shape
    quarter = s // DFT_RADIX
    cc, sc = _dft_tables(FOURIER_GROUP, FOURIER_GROUP, FOURIER_GROUP)
    u = pl.pallas_call(
        _chan_dft_body,
        out_shape=jax.ShapeDtypeStruct((b, DFT_RADIX, 2, quarter, D_MODEL), BF16),
        grid=(b, quarter // CHAN_DFT_ROWS),
        in_specs=[pl.BlockSpec((1, DFT_RADIX, CHAN_DFT_ROWS, D_MODEL),
                               lambda bi, i: (bi, 0, i, 0)),
                  _resident((1, D_MODEL)),
                  _resident((FOURIER_GROUP, FOURIER_GROUP)),
                  _resident((FOURIER_GROUP, FOURIER_GROUP))],
        out_specs=pl.BlockSpec((1, DFT_RADIX, 2, CHAN_DFT_ROWS, D_MODEL),
                               lambda bi, i: (bi, 0, 0, i, 0)),
        compiler_params=_params("parallel", "parallel"),
        name="fourier_chan_dft",
    )(x.reshape(b, DFT_RADIX, quarter, D_MODEL), gain.reshape(1, D_MODEL),
      cc.astype(BF16), (-sc).astype(BF16))
    u = u.reshape(b, DFT_RADIX, 2 * quarter, D_MODEL)
    inv_norm = 1.0 / math.sqrt(s * FOURIER_GROUP)
    tile = pl.BlockSpec((1, DFT_RADIX * SEQ_DFT_ROWS, D_MODEL), lambda bi, i: (bi, i, 0))
    return pl.pallas_call(
        functools.partial(_seq_dft_body, inv_norm=inv_norm),
        out_shape=jax.ShapeDtypeStruct(x.shape, F32),
        grid=(b, quarter // SEQ_DFT_ROWS),
        in_specs=[pl.BlockSpec((SEQ_DFT_ROWS, DFT_RADIX * 2 * quarter), lambda bi, i: (i, 0)),
                  pl.BlockSpec((1, DFT_RADIX, 2 * quarter, D_MODEL),
                               lambda bi, i: (bi, 0, 0, 0), pipeline_mode=pl.Buffered(1)),
                  tile,
                  _resident((D_MODEL, D_MODEL)),
                  _resident((1, D_MODEL)),
                  _resident((DFT_RADIX * SEQ_DFT_GROUP, DFT_RADIX * SEQ_DFT_GROUP))],
        out_specs=tile,
        compiler_params=_params("parallel", "arbitrary"),
        name="fourier_seq_dft",
    )(_seq_dft_matrix(s), u, x, w_out.astype(BF16), b_out.reshape(1, D_MODEL),
      _interleave_matrix())


def _rope_angles(s):
    rows = s // GRID_W
    row = jnp.repeat(jnp.arange(rows, dtype=F32), GRID_W)
    col = jnp.tile(jnp.arange(GRID_W, dtype=F32), rows)
    half = HEAD_DIM // 2
    inv_freq = ROPE_THETA ** (-jnp.arange(0, half, 2, dtype=F32) / half)
    ang_r = row[:, None] * inv_freq[None, :]
    ang_c = col[:, None] * inv_freq[None, :]
    return jnp.concatenate([ang_r, ang_r, ang_c, ang_c], axis=-1)


def _swap_quarters(a, axis):
    q0, q1, q2, q3 = jnp.split(a, 4, axis=axis)
    return jnp.concatenate([q1, q0, q3, q2], axis=axis)


def _qkv_body(x_ref, g_ref, wqv_ref, wk_ref, kg_ref, qcos_ref, qsin_ref,
              kcos_ref, ksup_ref, ksdn_ref, qt_ref, k_ref, vt_ref):
    h = _rms(x_ref[...], g_ref[...]).astype(BF16)
    rows = h.shape[0]
    k2 = _dot(h, wk_ref[...])
    kcos, ksup, ksdn = kcos_ref[...], ksup_ref[...], ksdn_ref[...]
    quarter = HEAD_DIM // 4
    for g in range(N_KV_HEADS):
        u = k2[:, g * HEAD_DIM:(g + 1) * HEAD_DIM]
        ms = jnp.mean(u * u, axis=-1, keepdims=True)
        u = u * lax.rsqrt(ms + NORM_EPS) * kg_ref[...]
        up = pltpu.roll(u, HEAD_DIM - quarter, 1)
        dn = pltpu.roll(u, quarter, 1)
        k_ref[:, g * HEAD_DIM:(g + 1) * HEAD_DIM] = (
            u * kcos + (up * ksup + dn * ksdn)).astype(BF16)
    ut = _dot_bt(wqv_ref[...], h)
    qcos, qsin = qcos_ref[...], qsin_ref[...]
    for hd in range(N_Q_HEADS):
        u = ut[hd * HEAD_DIM:(hd + 1) * HEAD_DIM, :]
        ms = jnp.mean(u * u, axis=0, keepdims=True)
        n = u * lax.rsqrt(ms + NORM_EPS)
        r = n * qcos + _swap_quarters(n, 0) * qsin
        qt_ref[hd * HEAD_DIM:(hd + 1) * HEAD_DIM, :] = r.astype(BF16)
    ones_row = lax.broadcasted_iota(jnp.int32, (V_AUG_ROWS - HEAD_DIM, rows), 0) == 0
    for g in range(N_KV_HEADS):
        v_t = ut[D_Q + g * HEAD_DIM:D_Q + (g + 1) * HEAD_DIM, :]
        vt_ref[g, 0:HEAD_DIM, :] = v_t.astype(BF16)
        vt_ref[g, HEAD_DIM:, :] = jnp.where(ones_row, 1.0, 0.0).astype(BF16)


def _attn_body(qt_ref, k_ref, vt_ref, o_ref, s_ref, *, seq):
    n_chunks = seq // ATTN_KV_ROWS
    groups = range(ATTN_COL_GROUPS)
    width = Q_PER_KV * ATTN_Q_ROWS // ATTN_COL_GROUPS

    def scores(c, g, q):
        st = _dot(k_ref[c * ATTN_KV_ROWS:(c + 1) * ATTN_KV_ROWS, :], q)
        s_ref[c % 2, :, g * width:(g + 1) * width] = st
        return jnp.max(st, axis=0, keepdims=True)

    def q_block(i, carry):
        r0 = pl.multiple_of(i * ATTN_Q_ROWS, ATTN_Q_ROWS)
        qb = qt_ref[:, pl.ds(r0, ATTN_Q_ROWS)]
        qcat = jnp.concatenate([qb[r * HEAD_DIM:(r + 1) * HEAD_DIM, :]
                                for r in range(Q_PER_KV)], axis=1)
        q = [qcat[:, g * width:(g + 1) * width] for g in groups]
        m = [None] * ATTN_COL_GROUPS
        acc = [None] * ATTN_COL_GROUPS
        cm = [scores(0, g, q[g]) for g in groups]
        for c in range(n_chunks):
            kv = slice(c * ATTN_KV_ROWS, (c + 1) * ATTN_KV_ROWS)
            for g in groups:
                cm_next = scores(c + 1, g, q[g]) if c + 1 < n_chunks else None
                m_new = cm[g] if m[g] is None else jnp.maximum(m[g], cm[g])
                pv = None
                for kt in range(ATTN_KV_ROWS // V7X_MXU_DIM):
                    rows = slice(kt * V7X_MXU_DIM, (kt + 1) * V7X_MXU_DIM)
                    p = jnp.exp2(s_ref[c % 2, rows, g * width:(g + 1) * width] - m_new).astype(BF16)
                    k0 = c * ATTN_KV_ROWS + kt * V7X_MXU_DIM
                    part = _dot(vt_ref[0, :, k0:k0 + V7X_MXU_DIM], p)
                    pv = part if pv is None else pv + part
                acc[g] = pv if m[g] is None else acc[g] * jnp.exp2(m[g] - m_new) + pv
                m[g], cm[g] = m_new, cm_next
        out_t = jnp.concatenate([a[0:HEAD_DIM] / a[HEAD_DIM:HEAD_DIM + 1] for a in acc], axis=1)
        for r in range(Q_PER_KV):
            head = out_t[:, r * ATTN_Q_ROWS:(r + 1) * ATTN_Q_ROWS].T
            o_ref[0, pl.ds(r0, ATTN_Q_ROWS), r * HEAD_DIM:(r + 1) * HEAD_DIM] = head.astype(BF16)
        return carry

    lax.fori_loop(0, seq // ATTN_Q_ROWS, q_block, 0)


def _attention_mixer(x, gain, w_qkv, q_gain, k_gain, w_o):
    b, s, _ = x.shape
    n = b * s
    x2d = x.reshape(n, D_MODEL)
    ang = _rope_angles(s)
    cos, sin = jnp.cos(ang), jnp.sin(ang)
    first = ((jnp.arange(HEAD_DIM) % (HEAD_DIM // 2)) < HEAD_DIM // 4)
    ksup = jnp.where(first[None, :], -sin, 0.0)
    ksdn = jnp.where(first[None, :], 0.0, sin)
    c = HEAD_DIM ** -0.5 * math.log2(math.e)
    sign = jnp.where(first, -1.0, 1.0)
    qcos = (q_gain * c)[:, None] * cos.T
    qsin = (_swap_quarters(q_gain, 0) * sign * c)[:, None] * sin.T
    wqv_t = jnp.concatenate([w_qkv[:, :D_Q], w_qkv[:, D_Q + D_KV:]], axis=1).T.astype(BF16)
    wk = w_qkv[:, D_Q:D_Q + D_KV].astype(BF16)
    tiles_per_seq = s // TOKEN_ROWS
    row = lambda width: pl.BlockSpec((TOKEN_ROWS, width), lambda i: (i, 0))
    pos = pl.BlockSpec((TOKEN_ROWS, HEAD_DIM), lambda i: (i % tiles_per_seq, 0))
    pos_t = pl.BlockSpec((HEAD_DIM, TOKEN_ROWS), lambda i: (0, i % tiles_per_seq))
    qt, k, vt = pl.pallas_call(
        _qkv_body,
        out_shape=(jax.ShapeDtypeStruct((D_Q, n), BF16),
                   jax.ShapeDtypeStruct((n, D_KV), BF16),
                   jax.ShapeDtypeStruct((N_KV_HEADS, V_AUG_ROWS, n), BF16)),
        grid=(n // TOKEN_ROWS,),
        in_specs=[row(D_MODEL), _resident((1, D_MODEL)),
                  _resident((D_Q + D_KV, D_MODEL)), _resident((D_MODEL, D_KV)),
                  _resident((1, HEAD_DIM)), pos_t, pos_t, pos, pos, pos],
        out_specs=(pl.BlockSpec((D_Q, TOKEN_ROWS), lambda i: (0, i)), row(D_KV),
                   pl.BlockSpec((N_KV_HEADS, V_AUG_ROWS, TOKEN_ROWS), lambda i: (0, 0, i))),
        compiler_params=_params("parallel"),
        name="attn_qkv",
    )(x2d, gain.reshape(1, D_MODEL), wqv_t, wk, k_gain.reshape(1, HEAD_DIM),
      qcos, qsin, cos, ksup, ksdn)

    group_w = Q_PER_KV * HEAD_DIM
    attn = pl.pallas_call(
        functools.partial(_attn_body, seq=s),
        out_shape=jax.ShapeDtypeStruct((b, s, D_Q), BF16),
        grid=(b, N_KV_HEADS),
        in_specs=[pl.BlockSpec((group_w, s), lambda bi, g: (g, bi)),
                  pl.BlockSpec((s, HEAD_DIM), lambda bi, g: (bi, g)),
                  pl.BlockSpec((1, V_AUG_ROWS, s), lambda bi, g: (g, 0, bi))],
        out_specs=pl.BlockSpec((1, s, group_w), lambda bi, g: (bi, 0, g)),
        scratch_shapes=[pltpu.VMEM((2, ATTN_KV_ROWS, Q_PER_KV * ATTN_Q_ROWS), F32)],
        compiler_params=_params("parallel", "parallel"),
        name="attn_core",
    )(qt, k, vt)
    return attn.reshape(n, D_Q), w_o.astype(BF16)


def kernel(x, ffn1_norm, ffn1_w_gate, ffn1_w_up, ffn1_w_down, mixer_norm, ffn2_norm, ffn2_w_gate, ffn2_w_up, ffn2_w_down, pool_w, pool_b, pool_scale, fourier_w, fourier_b, attn_w_qkv, attn_q_norm, attn_k_norm, attn_w_o, final_norm):
    b, s, d = x.shape
    depth = ffn1_norm.shape[0]
    n = b * s
    ffn1 = (ffn1_norm.reshape(depth, 1, d), ffn1_w_gate.astype(BF16),
            ffn1_w_up.astype(BF16), ffn1_w_down.astype(BF16))
    ffn2 = (ffn2_norm.reshape(depth, 1, d), ffn2_w_gate.astype(BF16),
            ffn2_w_up.astype(BF16), ffn2_w_down.astype(BF16))

    def ffn(x3d, layer, params, final_gain=None, proj=None):
        out = _ffn(x3d.reshape(n, d), layer, *params, final_gain=final_gain, proj=proj)
        return out.reshape(b, s, d)

    for i in range(depth):
        x = ffn(x, i, ffn1)
        kind, j = i % N_MIXERS, i // N_MIXERS
        proj = None
        final_gain = final_norm if i == depth - 1 else None
        if kind == 0:
            x = _pool_ffn(x, mixer_norm[i], pool_w[j], pool_b[j], pool_scale[j], i, *ffn2,
                          final_gain=final_gain)
            continue
        if kind == 1:
            x = _fourier_mixer(x, mixer_norm[i], fourier_w[j], fourier_b[j])
        else:
            proj = _attention_mixer(x, mixer_norm[i], attn_w_qkv[j], attn_q_norm[j],
                                    attn_k_norm[j], attn_w_o[j])
        x = ffn(x, i, ffn2, final_gain, proj)
    return x
```

```python
import functools
import math

import jax
import jax.numpy as jnp
from jax import lax
from jax.experimental import pallas as pl
from jax.experimental.pallas import tpu as pltpu

D_MODEL = 1024
D_FF = 2816
N_MIXERS = 3
NORM_EPS = 1e-6
POOL_WINDOWS = (2, 4, 8, 16)
POOL_GROUP = D_MODEL // len(POOL_WINDOWS)
POOL_HALO = max(POOL_WINDOWS) // 2
N_FOURIER_GROUPS = 4
FOURIER_GROUP = D_MODEL // N_FOURIER_GROUPS
DFT_RADIX = 4
HEAD_DIM = 128
N_Q_HEADS = D_MODEL // HEAD_DIM
N_KV_HEADS = N_Q_HEADS // 4
Q_PER_KV = N_Q_HEADS // N_KV_HEADS
D_Q = N_Q_HEADS * HEAD_DIM
D_KV = N_KV_HEADS * HEAD_DIM
V_AUG_ROWS = HEAD_DIM + 16
GRID_W = 64
ROPE_THETA = 10000.0

V7X_MXU_DIM = 256
V7X_BF16_ROWS = 16
V7X_VMEM_BYTES = 64 * 1024 * 1024
VMEM_LIMIT_BYTES = V7X_VMEM_BYTES * 7 // 8

FFN_ROWS = 1024
FFN_COLS = V7X_MXU_DIM
POOL_ROWS = 512
POOL_PAD = V7X_BF16_ROWS
TOKEN_ROWS = 512
CHAN_DFT_ROWS = 256
SEQ_DFT_ROWS = V7X_MXU_DIM
SEQ_DFT_GROUP = V7X_MXU_DIM // DFT_RADIX
ATTN_Q_ROWS = 512
ATTN_KV_ROWS = 512
ATTN_COL_GROUPS = 8

F32 = jnp.float32
BF16 = jnp.bfloat16


def _params(*semantics):
    return pltpu.CompilerParams(dimension_semantics=semantics,
                                vmem_limit_bytes=VMEM_LIMIT_BYTES)


def _resident(shape, index=None):
    lead = tuple(index if d is None else 0 for d in shape)
    return pl.BlockSpec(shape, lambda *_: lead, pipeline_mode=pl.Buffered(1))


def _rms(x, gain):
    ms = jnp.mean(x * x, axis=-1, keepdims=True)
    return x * lax.rsqrt(ms + NORM_EPS) * gain


def _dot(a, b):
    return jnp.dot(a, b, preferred_element_type=F32)


def _dot_bt(a, b):
    return lax.dot_general(a, b, (((1,), (1,)), ((), ())), preferred_element_type=F32)


def _swiglu_into(h, wg_ref, wu_ref, act_ref, side_work=()):
    n_chunks = D_FF // FFN_COLS
    after = {((k + 1) * n_chunks) // (len(side_work) + 1) - 1: piece
             for k, piece in enumerate(side_work)}
    for c in range(n_chunks):
        cols = slice(c * FFN_COLS, (c + 1) * FFN_COLS)
        gate = _dot(h[...], wg_ref[:, cols])
        up = _dot(h[...], wu_ref[:, cols])
        act_ref[:, cols] = (gate * jax.nn.sigmoid(gate) * up).astype(BF16)
        if c in after:
            after[c]()


class _CastAhead:
    def __init__(self, steps, stacked_f32, layer):
        self.args = list(stacked_f32)

        def slabs(total_rows, cols):
            n = 1
            while n * 2 <= steps and total_rows % (n * 2 * V7X_BF16_ROWS) == 0:
                n *= 2
            return total_rows // n, cols, n

        self.in_specs, self.out_specs, self.out_shapes = [], [], []
        for rows, cols, n_blocks in (slabs(D_MODEL, D_FF), slabs(D_MODEL, D_FF),
                                     slabs(D_FF, D_MODEL)):
            last = n_blocks - 1
            self.in_specs.append(pl.BlockSpec(
                (None, rows, cols), lambda i, last=last: (layer, jnp.minimum(i, last), 0)))
            self.out_specs.append(pl.BlockSpec(
                (rows, cols), lambda i, last=last: (jnp.minimum(i, last), 0)))
            self.out_shapes.append(jax.ShapeDtypeStruct((rows * n_blocks, cols), BF16))

    @staticmethod
    def run(src_refs, dst_refs):
        for src, dst in zip(src_refs, dst_refs):
            dst[...] = src[...].astype(BF16)


def _ffn_body(*refs, final_norm, pre_proj, cast_ahead):
    refs = list(refs)
    act_ref = refs.pop()
    side_work = []
    if cast_ahead:
        dst_refs = [refs.pop() for _ in range(3)][::-1]
        o_ref = refs.pop()
        src_refs = [refs.pop() for _ in range(3)][::-1]
        side_work.append(functools.partial(_CastAhead.run, src_refs, dst_refs))
    else:
        o_ref = refs.pop()
    x_ref = refs.pop(0)
    x = x_ref[...]
    if pre_proj:
        a_ref, wp_ref = refs.pop(0), refs.pop(0)
        x = x + _dot(a_ref[...], wp_ref[...])
    g_ref, wg_ref, wu_ref, wd_ref = refs[:4]
    _swiglu_into(_rms(x, g_ref[...]).astype(BF16), wg_ref, wu_ref, act_ref, side_work)
    out = x + 0.5 * _dot(act_ref[...], wd_ref[...])
    if final_norm:
        out = _rms(out, refs[4][...])
    o_ref[...] = out


def _ffn(x2d, gain, wg, wu, wd, final_gain=None, proj=None, next_weights=None):
    n = x2d.shape[0]
    steps = n // FFN_ROWS
    row_spec = pl.BlockSpec((FFN_ROWS, D_MODEL), lambda i: (i, 0))
    in_specs, args = [row_spec], [x2d]
    if proj is not None:
        a, w_proj = proj
        in_specs += [pl.BlockSpec((FFN_ROWS, a.shape[1]), lambda i: (i, 0)),
                     _resident(w_proj.shape)]
        args += [a, w_proj]
    in_specs += [_resident(gain.shape), _resident(wg.shape), _resident(wu.shape),
                 _resident(wd.shape)]
    args += [gain, wg, wu, wd]
    if final_gain is not None:
        in_specs.append(_resident((1, D_MODEL)))
        args.append(final_gain.reshape(1, D_MODEL))
    out_specs, out_shapes = [row_spec], [jax.ShapeDtypeStruct(x2d.shape, F32)]
    if next_weights is not None:
        cast = _CastAhead(steps, *next_weights)
        in_specs += cast.in_specs
        args += cast.args
        out_specs += cast.out_specs
        out_shapes += cast.out_shapes
    name = "ffn" + ("_proj" if proj is not None else "") + ("_final" if final_gain is not None else "")
    outs = pl.pallas_call(
        functools.partial(_ffn_body, final_norm=final_gain is not None,
                          pre_proj=proj is not None, cast_ahead=next_weights is not None),
        out_shape=tuple(out_shapes),
        grid=(steps,),
        in_specs=in_specs,
        out_specs=tuple(out_specs),
        scratch_shapes=[pltpu.VMEM((FFN_ROWS, D_FF), BF16)],
        compiler_params=_params("arbitrary"),
        name=name,
    )(*args)
    return outs[0], tuple(outs[1:])


def _pool_prepare(xm_ref, xp_ref, xn_ref, g_ref, ext_ref, *, i, n_i):
    gain = g_ref[...]
    fill = jnp.zeros((POOL_PAD - POOL_HALO, D_MODEL), F32)
    hp = jnp.where(i > 0, _rms(xp_ref[0], gain), 0.0)
    hn = jnp.where(i < n_i - 1, _rms(xn_ref[0], gain), 0.0)
    ext_ref[0:POOL_PAD] = jnp.concatenate([fill, hp], axis=0)
    ext_ref[POOL_PAD:POOL_PAD + POOL_ROWS] = _rms(xm_ref[0], gain)
    ext_ref[POOL_PAD + POOL_ROWS:] = jnp.concatenate([hn, fill], axis=0)


def _pool_group(g, xm_ref, pw_ref, pb_ref, ps_ref, o_ref, ext_ref, *, seq, i):
    w = POOL_WINDOWS[g]
    cols = slice(g * POOL_GROUP, (g + 1) * POOL_GROUP)
    wsum = None
    for k in range(-(w // 2), w // 2):
        shifted = ext_ref[POOL_PAD + k:POOL_PAD + k + POOL_ROWS, cols]
        wsum = shifted if wsum is None else wsum + shifted
    t = i * POOL_ROWS + lax.broadcasted_iota(jnp.int32, (POOL_ROWS, 1), 0)
    cnt = jnp.minimum(t + w // 2, seq) - jnp.maximum(t - w // 2, 0)
    hm = ext_ref[POOL_PAD:POOL_PAD + POOL_ROWS, cols]
    pooled = wsum / cnt.astype(F32) - hm
    y = _dot(pooled.astype(BF16), pw_ref[g]) + pb_ref[g]
    o_ref[:, cols] = xm_ref[0, :, cols] + y * ps_ref[:, cols]


def _pool_ffn_body(xm_ref, xp_ref, xn_ref, mg_ref, pw_ref, pb_ref, ps_ref,
                   g_ref, wg_ref, wu_ref, wd_ref, *rest, seq, final_norm, cast_ahead):
    rest = list(rest)
    fg_ref = rest.pop(0) if final_norm else None
    ext_ref, mid_ref, h_ref, act_ref = rest[-4:]
    side_work = []
    if cast_ahead:
        src_refs, o_ref, dst_refs = rest[0:3], rest[3], rest[4:7]
        side_work.append(functools.partial(_CastAhead.run, src_refs, dst_refs))
    else:
        o_ref = rest[0]
    j = pl.program_id(0)
    n_tiles = pl.num_programs(0) - 1
    tiles_per_seq = seq // POOL_ROWS

    @pl.when(j == 0)
    def _():
        mid_ref[...] = jnp.zeros_like(mid_ref)

    x = mid_ref[...]
    o_ref[...] = x
    h_ref[...] = _rms(x, g_ref[...]).astype(BF16)
    i = jnp.minimum(j, n_tiles - 1) % tiles_per_seq
    pool_pieces = [functools.partial(_pool_prepare, xm_ref, xp_ref, xn_ref, mg_ref, ext_ref,
                                     i=i, n_i=tiles_per_seq)]
    pool_pieces += [functools.partial(_pool_group, g, xm_ref, pw_ref, pb_ref, ps_ref,
                                      mid_ref, ext_ref, seq=seq, i=i)
                    for g in range(len(POOL_WINDOWS))]
    _swiglu_into(h_ref, wg_ref, wu_ref, act_ref, side_work=pool_pieces + side_work)
    out = o_ref[...] + 0.5 * _dot(act_ref[...], wd_ref[...])
    if final_norm:
        out = _rms(out, fg_ref[...])
    o_ref[...] = out


def _pool_ffn(x, mixer_gain, pw, pb, ps, gain, wg, wu, wd, final_gain=None, next_weights=None):
    b, s, _ = x.shape
    tiles_per_seq = s // POOL_ROWS
    n_tiles = b * tiles_per_seq
    blocks_per_tile = POOL_ROWS // POOL_HALO
    n_halo_blocks = s // POOL_HALO
    n_groups = len(POOL_WINDOWS)

    def tile(j):
        jp = jnp.minimum(j, n_tiles - 1)
        return jp // tiles_per_seq, jp % tiles_per_seq

    def main_map(j):
        bi, i = tile(j)
        return bi, i, 0

    def prev_map(j):
        bi, i = tile(j)
        return bi, jnp.maximum(i * blocks_per_tile - 1, 0), 0

    def next_map(j):
        bi, i = tile(j)
        return bi, jnp.minimum((i + 1) * blocks_per_tile, n_halo_blocks - 1), 0

    in_specs = [
        pl.BlockSpec((1, POOL_ROWS, D_MODEL), main_map),
        pl.BlockSpec((1, POOL_HALO, D_MODEL), prev_map),
        pl.BlockSpec((1, POOL_HALO, D_MODEL), next_map),
        _resident((1, D_MODEL)),
        _resident((n_groups, POOL_GROUP, POOL_GROUP)),
        _resident((n_groups, 1, POOL_GROUP)),
        _resident((1, D_MODEL)),
        _resident(gain.shape), _resident(wg.shape), _resident(wu.shape), _resident(wd.shape),
    ]
    args = [x, x, x, mixer_gain.reshape(1, D_MODEL), pw.astype(BF16),
            pb.reshape(n_groups, 1, POOL_GROUP), ps.reshape(1, D_MODEL), gain, wg, wu, wd]
    if final_gain is not None:
        in_specs.append(_resident((1, D_MODEL)))
        args.append(final_gain.reshape(1, D_MODEL))
    out_specs = [pl.BlockSpec((POOL_ROWS, D_MODEL), lambda j: (jnp.maximum(j - 1, 0), 0))]
    out_shapes = [jax.ShapeDtypeStruct((b * s, D_MODEL), F32)]
    if next_weights is not None:
        cast = _CastAhead(n_tiles + 1, *next_weights)
        in_specs += cast.in_specs
        args += cast.args
        out_specs += cast.out_specs
        out_shapes += cast.out_shapes
    outs = pl.pallas_call(
        functools.partial(_pool_ffn_body, seq=s, final_norm=final_gain is not None,
                          cast_ahead=next_weights is not None),
        out_shape=tuple(out_shapes),
        grid=(n_tiles + 1,),
        in_specs=in_specs,
        out_specs=tuple(out_specs),
        scratch_shapes=[pltpu.VMEM((POOL_ROWS + 2 * POOL_PAD, D_MODEL), F32),
                        pltpu.VMEM((POOL_ROWS, D_MODEL), F32),
                        pltpu.VMEM((POOL_ROWS, D_MODEL), BF16),
                        pltpu.VMEM((POOL_ROWS, D_FF), BF16)],
        compiler_params=_params("arbitrary"),
        name="pool_ffn_final" if final_gain is not None else "pool_ffn",
    )(*args)
    return outs[0].reshape(b, s, D_MODEL), tuple(outs[1:])


def _dft_tables(n_rows, n_cols, n):
    k = (jnp.arange(n_rows, dtype=jnp.int32)[:, None]
         * jnp.arange(n_cols, dtype=jnp.int32)[None, :]) % n
    ang = k.astype(F32) * (2.0 * math.pi / n)
    return jnp.cos(ang), jnp.sin(ang)


def _seq_dft_matrix(s):
    quarter = s // DFT_RADIX
    root = int(math.isqrt(quarter))
    assert root * root == quarter
    j = jnp.arange(DFT_RADIX * 2 * quarter, dtype=jnp.int32)
    r, part, t = j // (2 * quarter), (j // quarter) % 2, j % quarter
    beta = ((t * r) % s).astype(F32) * (2.0 * math.pi / s) - part.astype(F32) * (math.pi / 2)
    rows = jnp.arange(root, dtype=jnp.int32)[:, None]
    ang_a = ((rows * t[None, :]) % root).astype(F32) * (2.0 * math.pi / root)
    ang_b = ((rows * t[None, :]) % quarter).astype(F32) * (2.0 * math.pi / quarter) + beta[None, :]
    w = (jnp.cos(ang_a)[:, None, :] * jnp.cos(ang_b)[None, :, :]
         - jnp.sin(ang_a)[:, None, :] * jnp.sin(ang_b)[None, :, :])
    return w.astype(BF16).reshape(quarter, DFT_RADIX * 2 * quarter)


def _chan_dft_body(x_ref, g_ref, cc_ref, nsc_ref, u_ref):
    gain = g_ref[...]
    z_re, z_im = [], []
    for q in range(DFT_RADIX):
        h = _rms(x_ref[0, q], gain).astype(BF16)
        re, im = [], []
        for g in range(N_FOURIER_GROUPS):
            cols = slice(g * FOURIER_GROUP, (g + 1) * FOURIER_GROUP)
            re.append(_dot(h[:, cols], cc_ref[...]))
            im.append(_dot(h[:, cols], nsc_ref[...]))
        z_re.append(jnp.concatenate(re, axis=-1))
        z_im.append(jnp.concatenate(im, axis=-1))
    t0 = (z_re[0] + z_re[2], z_im[0] + z_im[2])
    t1 = (z_re[0] - z_re[2], z_im[0] - z_im[2])
    t2 = (z_re[1] + z_re[3], z_im[1] + z_im[3])
    t3 = (z_re[1] - z_re[3], z_im[1] - z_im[3])
    u = [(t0[0] + t2[0], t0[1] + t2[1]),
         (t1[0] + t3[1], t1[1] - t3[0]),
         (t0[0] - t2[0], t0[1] - t2[1]),
         (t1[0] - t3[1], t1[1] + t3[0])]
    for r in range(DFT_RADIX):
        u_ref[0, r, 0] = u[r][0].astype(BF16)
        u_ref[0, r, 1] = u[r][1].astype(BF16)


def _interleave_matrix():
    out_row = jnp.arange(DFT_RADIX * SEQ_DFT_GROUP)[:, None]
    in_row = jnp.arange(DFT_RADIX * SEQ_DFT_GROUP)[None, :]
    src = (out_row % DFT_RADIX) * SEQ_DFT_GROUP + out_row // DFT_RADIX
    return (in_row == src).astype(BF16)


def _seq_dft_body(w_ref, u_ref, x_ref, wo_ref, bo_ref, perm_ref, o_ref, *, inv_norm):
    rows, two_q = w_ref.shape[0], u_ref.shape[2]
    f = [(_dot(w_ref[:, r * two_q:(r + 1) * two_q], u_ref[0, r]) * inv_norm).astype(BF16)
         for r in range(DFT_RADIX)]
    span = DFT_RADIX * SEQ_DFT_GROUP
    for j in range(rows // SEQ_DFT_GROUP):
        grp = slice(j * SEQ_DFT_GROUP, (j + 1) * SEQ_DFT_GROUP)
        stacked = jnp.concatenate([f[r][grp] for r in range(DFT_RADIX)], axis=0)
        f_tok = _dot(perm_ref[...], stacked).astype(BF16)
        tok = slice(j * span, (j + 1) * span)
        o_ref[0, tok, :] = x_ref[0, tok, :] + _dot(f_tok, wo_ref[...]) + bo_ref[...]


def _fourier_mixer(x, gain, w_out, b_out):
    b, s, _ = x.shape
    quarter = s // DFT_RADIX
    cc, sc = _dft_tables(FOURIER_GROUP, FOURIER_GROUP, FOURIER_GROUP)
    u = pl.pallas_call(
        _chan_dft_body,
        out_shape=jax.ShapeDtypeStruct((b, DFT_RADIX, 2, quarter, D_MODEL), BF16),
        grid=(b, quarter // CHAN_DFT_ROWS),
        in_specs=[pl.BlockSpec((1, DFT_RADIX, CHAN_DFT_ROWS, D_MODEL),
                               lambda bi, i: (bi, 0, i, 0)),
                  _resident((1, D_MODEL)),
                  _resident((FOURIER_GROUP, FOURIER_GROUP)),
                  _resident((FOURIER_GROUP, FOURIER_GROUP))],
        out_specs=pl.BlockSpec((1, DFT_RADIX, 2, CHAN_DFT_ROWS, D_MODEL),
                               lambda bi, i: (bi, 0, 0, i, 0)),
        compiler_params=_params("parallel", "parallel"),
        name="fourier_chan_dft",
    )(x.reshape(b, DFT_RADIX, quarter, D_MODEL), gain.reshape(1, D_MODEL),
      cc.astype(BF16), (-sc).astype(BF16))
    u = u.reshape(b, DFT_RADIX, 2 * quarter, D_MODEL)
    inv_norm = 1.0 / math.sqrt(s * FOURIER_GROUP)
    tile = pl.BlockSpec((1, DFT_RADIX * SEQ_DFT_ROWS, D_MODEL), lambda bi, i: (bi, i, 0))
    return pl.pallas_call(
        functools.partial(_seq_dft_body, inv_norm=inv_norm),
        out_shape=jax.ShapeDtypeStruct(x.shape, F32),
        grid=(b, quarter // SEQ_DFT_ROWS),
        in_specs=[pl.BlockSpec((SEQ_DFT_ROWS, DFT_RADIX * 2 * quarter), lambda bi, i: (i, 0)),
                  pl.BlockSpec((1, DFT_RADIX, 2 * quarter, D_MODEL),
                               lambda bi, i: (bi, 0, 0, 0), pipeline_mode=pl.Buffered(1)),
                  tile,
                  _resident((D_MODEL, D_MODEL)),
                  _resident((1, D_MODEL)),
                  _resident((DFT_RADIX * SEQ_DFT_GROUP, DFT_RADIX * SEQ_DFT_GROUP))],
        out_specs=tile,
        compiler_params=_params("parallel", "arbitrary"),
        name="fourier_seq_dft",
    )(_seq_dft_matrix(s), u, x, w_out.astype(BF16), b_out.reshape(1, D_MODEL),
      _interleave_matrix())


def _rope_angles(s):
    rows = s // GRID_W
    row = jnp.repeat(jnp.arange(rows, dtype=F32), GRID_W)
    col = jnp.tile(jnp.arange(GRID_W, dtype=F32), rows)
    half = HEAD_DIM // 2
    inv_freq = ROPE_THETA ** (-jnp.arange(0, half, 2, dtype=F32) / half)
    ang_r = row[:, None] * inv_freq[None, :]
    ang_c = col[:, None] * inv_freq[None, :]
    return jnp.concatenate([ang_r, ang_r, ang_c, ang_c], axis=-1)


def _swap_quarters(a, axis):
    q0, q1, q2, q3 = jnp.split(a, 4, axis=axis)
    return jnp.concatenate([q1, q0, q3, q2], axis=axis)


def _qkv_body(x_ref, g_ref, wqv_ref, wk_ref, kg_ref, qcos_ref, qsin_ref,
              kcos_ref, ksup_ref, ksdn_ref, qt_ref, k_ref, vt_ref):
    h = _rms(x_ref[...], g_ref[...]).astype(BF16)
    rows = h.shape[0]
    k2 = _dot(h, wk_ref[...])
    kcos, ksup, ksdn = kcos_ref[...], ksup_ref[...], ksdn_ref[...]
    quarter = HEAD_DIM // 4
    for g in range(N_KV_HEADS):
        u = k2[:, g * HEAD_DIM:(g + 1) * HEAD_DIM]
        ms = jnp.mean(u * u, axis=-1, keepdims=True)
        u = u * lax.rsqrt(ms + NORM_EPS) * kg_ref[...]
        up = pltpu.roll(u, HEAD_DIM - quarter, 1)
        dn = pltpu.roll(u, quarter, 1)
        k_ref[:, g * HEAD_DIM:(g + 1) * HEAD_DIM] = (
            u * kcos + (up * ksup + dn * ksdn)).astype(BF16)
    ut = _dot_bt(wqv_ref[...], h)
    qcos, qsin = qcos_ref[...], qsin_ref[...]
    for hd in range(N_Q_HEADS):
        u = ut[hd * HEAD_DIM:(hd + 1) * HEAD_DIM, :]
        ms = jnp.mean(u * u, axis=0, keepdims=True)
        n = u * lax.rsqrt(ms + NORM_EPS)
        r = n * qcos + _swap_quarters(n, 0) * qsin
        qt_ref[hd * HEAD_DIM:(hd + 1) * HEAD_DIM, :] = r.astype(BF16)
    ones_row = lax.broadcasted_iota(jnp.int32, (V_AUG_ROWS - HEAD_DIM, rows), 0) == 0
    for g in range(N_KV_HEADS):
        v_t = ut[D_Q + g * HEAD_DIM:D_Q + (g + 1) * HEAD_DIM, :]
        vt_ref[g, 0:HEAD_DIM, :] = v_t.astype(BF16)
        vt_ref[g, HEAD_DIM:, :] = jnp.where(ones_row, 1.0, 0.0).astype(BF16)


def _attn_body(qt_ref, k_ref, vt_ref, o_ref, s_ref, *, seq):
    n_chunks = seq // ATTN_KV_ROWS
    groups = range(ATTN_COL_GROUPS)
    width = Q_PER_KV * ATTN_Q_ROWS // ATTN_COL_GROUPS

    def scores(c, g, q):
        st = _dot(k_ref[c * ATTN_KV_ROWS:(c + 1) * ATTN_KV_ROWS, :], q)
        s_ref[c % 2, :, g * width:(g + 1) * width] = st
        return jnp.max(st, axis=0, keepdims=True)

    def q_block(i, carry):
        r0 = pl.multiple_of(i * ATTN_Q_ROWS, ATTN_Q_ROWS)
        qb = qt_ref[:, pl.ds(r0, ATTN_Q_ROWS)]
        qcat = jnp.concatenate([qb[r * HEAD_DIM:(r + 1) * HEAD_DIM, :]
                                for r in range(Q_PER_KV)], axis=1)
        q = [qcat[:, g * width:(g + 1) * width] for g in groups]
        m = [None] * ATTN_COL_GROUPS
        acc = [None] * ATTN_COL_GROUPS
        cm = [scores(0, g, q[g]) for g in groups]
        for c in range(n_chunks):
            kv = slice(c * ATTN_KV_ROWS, (c + 1) * ATTN_KV_ROWS)
            for g in groups:
                cm_next = scores(c + 1, g, q[g]) if c + 1 < n_chunks else None
                m_new = cm[g] if m[g] is None else jnp.maximum(m[g], cm[g])
                pv = None
                for kt in range(ATTN_KV_ROWS // V7X_MXU_DIM):
                    rows = slice(kt * V7X_MXU_DIM, (kt + 1) * V7X_MXU_DIM)
                    p = jnp.exp2(s_ref[c % 2, rows, g * width:(g + 1) * width] - m_new).astype(BF16)
                    k0 = c * ATTN_KV_ROWS + kt * V7X_MXU_DIM
                    part = _dot(vt_ref[0, :, k0:k0 + V7X_MXU_DIM], p)
                    pv = part if pv is None else pv + part
                acc[g] = pv if m[g] is None else acc[g] * jnp.exp2(m[g] - m_new) + pv
                m[g], cm[g] = m_new, cm_next
        out_t = jnp.concatenate([a[0:HEAD_DIM] / a[HEAD_DIM:HEAD_DIM + 1] for a in acc], axis=1)
        for r in range(Q_PER_KV):
            head = out_t[:, r * ATTN_Q_ROWS:(r + 1) * ATTN_Q_ROWS].T
            o_ref[0, pl.ds(r0, ATTN_Q_ROWS), r * HEAD_DIM:(r + 1) * HEAD_DIM] = head.astype(BF16)
        return carry

    lax.fori_loop(0, seq // ATTN_Q_ROWS, q_block, 0)


def _attention_mixer(x, gain, w_qkv, q_gain, k_gain, w_o):
    b, s, _ = x.shape
    n = b * s
    x2d = x.reshape(n, D_MODEL)
    ang = _rope_angles(s)
    cos, sin = jnp.cos(ang), jnp.sin(ang)
    first = ((jnp.arange(HEAD_DIM) % (HEAD_DIM // 2)) < HEAD_DIM // 4)
    ksup = jnp.where(first[None, :], -sin, 0.0)
    ksdn = jnp.where(first[None, :], 0.0, sin)
    c = HEAD_DIM ** -0.5 * math.log2(math.e)
    sign = jnp.where(first, -1.0, 1.0)
    qcos = (q_gain * c)[:, None] * cos.T
    qsin = (_swap_quarters(q_gain, 0) * sign * c)[:, None] * sin.T
    wqv_t = jnp.concatenate([w_qkv[:, :D_Q], w_qkv[:, D_Q + D_KV:]], axis=1).T.astype(BF16)
    wk = w_qkv[:, D_Q:D_Q + D_KV].astype(BF16)
    tiles_per_seq = s // TOKEN_ROWS
    row = lambda width: pl.BlockSpec((TOKEN_ROWS, width), lambda i: (i, 0))
    pos = pl.BlockSpec((TOKEN_ROWS, HEAD_DIM), lambda i: (i % tiles_per_seq, 0))
    pos_t = pl.BlockSpec((HEAD_DIM, TOKEN_ROWS), lambda i: (0, i % tiles_per_seq))
    qt, k, vt = pl.pallas_call(
        _qkv_body,
        out_shape=(jax.ShapeDtypeStruct((D_Q, n), BF16),
                   jax.ShapeDtypeStruct((n, D_KV), BF16),
                   jax.ShapeDtypeStruct((N_KV_HEADS, V_AUG_ROWS, n), BF16)),
        grid=(n // TOKEN_ROWS,),
        in_specs=[row(D_MODEL), _resident((1, D_MODEL)),
                  _resident((D_Q + D_KV, D_MODEL)), _resident((D_MODEL, D_KV)),
                  _resident((1, HEAD_DIM)), pos_t, pos_t, pos, pos, pos],
        out_specs=(pl.BlockSpec((D_Q, TOKEN_ROWS), lambda i: (0, i)), row(D_KV),
                   pl.BlockSpec((N_KV_HEADS, V_AUG_ROWS, TOKEN_ROWS), lambda i: (0, 0, i))),
        compiler_params=_params("parallel"),
        name="attn_qkv",
    )(x2d, gain.reshape(1, D_MODEL), wqv_t, wk, k_gain.reshape(1, HEAD_DIM),
      qcos, qsin, cos, ksup, ksdn)

    group_w = Q_PER_KV * HEAD_DIM
    attn = pl.pallas_call(
        functools.partial(_attn_body, seq=s),
        out_shape=jax.ShapeDtypeStruct((b, s, D_Q), BF16),
        grid=(b, N_KV_HEADS),
        in_specs=[pl.BlockSpec((group_w, s), lambda bi, g: (g, bi)),
                  pl.BlockSpec((s, HEAD_DIM), lambda bi, g: (bi, g)),
                  pl.BlockSpec((1, V_AUG_ROWS, s), lambda bi, g: (g, 0, bi))],
        out_specs=pl.BlockSpec((1, s, group_w), lambda bi, g: (bi, 0, g)),
        scratch_shapes=[pltpu.VMEM((2, ATTN_KV_ROWS, Q_PER_KV * ATTN_Q_ROWS), F32)],
        compiler_params=_params("parallel", "parallel"),
        name="attn_core",
    )(qt, k, vt)
    return attn.reshape(n, D_Q), w_o.astype(BF16)


def kernel(x, ffn1_norm, ffn1_w_gate, ffn1_w_up, ffn1_w_down, mixer_norm, ffn2_norm, ffn2_w_gate, ffn2_w_up, ffn2_w_down, pool_w, pool_b, pool_scale, fourier_w, fourier_b, attn_w_qkv, attn_q_norm, attn_k_norm, attn_w_o, final_norm):
    b, s, d = x.shape
    depth = ffn1_norm.shape[0]
    n = b * s
    ffn1_f32 = (ffn1_w_gate, ffn1_w_up, ffn1_w_down)
    ffn2_f32 = (ffn2_w_gate, ffn2_w_up, ffn2_w_down)
    weights = tuple(w[0].astype(BF16) for w in ffn1_f32)

    def ffn(x3d, gain, weights, **kw):
        out, next_w = _ffn(x3d.reshape(n, d), gain.reshape(1, d), *weights, **kw)
        return out.reshape(b, s, d), next_w

    for i in range(depth):
        x, weights = ffn(x, ffn1_norm[i], weights, next_weights=(ffn2_f32, i))
        kind, j = i % N_MIXERS, i // N_MIXERS
        final_gain = final_norm if i == depth - 1 else None
        following = (ffn1_f32, i + 1) if i + 1 < depth else None
        if kind == 0:
            x, weights = _pool_ffn(x, mixer_norm[i], pool_w[j], pool_b[j], pool_scale[j],
                                   ffn2_norm[i].reshape(1, d), *weights,
                                   final_gain=final_gain, next_weights=following)
            continue
        proj = None
        if kind == 1:
            x = _fourier_mixer(x, mixer_norm[i], fourier_w[j], fourier_b[j])
        else:
            proj = _attention_mixer(x, mixer_norm[i], attn_w_qkv[j], attn_q_norm[j],
                                    attn_k_norm[j], attn_w_o[j])
        x, weights = ffn(x, ffn2_norm[i], weights, final_gain=final_gain, proj=proj,
                         next_weights=following)
    return x
```

```python
import functools
import math

import jax
import jax.numpy as jnp
from jax import lax
from jax.experimental import pallas as pl
from jax.experimental.pallas import tpu as pltpu

D_MODEL = 1024
D_FF = 2816
N_MIXERS = 3
NORM_EPS = 1e-6
POOL_WINDOWS = (2, 4, 8, 16)
POOL_GROUP = D_MODEL // len(POOL_WINDOWS)
POOL_HALO = max(POOL_WINDOWS) // 2
N_FOURIER_GROUPS = 4
FOURIER_GROUP = D_MODEL // N_FOURIER_GROUPS
DFT_RADIX = 4
HEAD_DIM = 128
N_Q_HEADS = D_MODEL // HEAD_DIM
N_KV_HEADS = N_Q_HEADS // 4
Q_PER_KV = N_Q_HEADS // N_KV_HEADS
D_Q = N_Q_HEADS * HEAD_DIM
D_KV = N_KV_HEADS * HEAD_DIM
V_AUG_ROWS = HEAD_DIM + 16
GRID_W = 64
ROPE_THETA = 10000.0

V7X_MXU_DIM = 256
V7X_BF16_ROWS = 16
V7X_VMEM_BYTES = 64 * 1024 * 1024
VMEM_LIMIT_BYTES = V7X_VMEM_BYTES * 7 // 8

FFN_ROWS = 1024
FFN_COLS = V7X_MXU_DIM
POOL_ROWS = 512
POOL_PAD = V7X_BF16_ROWS
TOKEN_ROWS = 512
CHAN_DFT_ROWS = 512
SEQ_DFT_ROWS = V7X_MXU_DIM
SEQ_DFT_GROUP = V7X_MXU_DIM // DFT_RADIX
ATTN_Q_ROWS = 512
ATTN_KV_ROWS = 512
ATTN_COL_GROUPS = 8

F32 = jnp.float32
BF16 = jnp.bfloat16


def _params(*semantics):
    return pltpu.CompilerParams(dimension_semantics=semantics,
                                vmem_limit_bytes=VMEM_LIMIT_BYTES)


def _resident(shape, index=None):
    lead = tuple(index if d is None else 0 for d in shape)
    return pl.BlockSpec(shape, lambda *_: lead, pipeline_mode=pl.Buffered(1))


def _rms(x, gain):
    ms = jnp.mean(x * x, axis=-1, keepdims=True)
    return x * lax.rsqrt(ms + NORM_EPS) * gain


def _dot(a, b):
    return jnp.dot(a, b, preferred_element_type=F32)


def _dot_bt(a, b):
    return lax.dot_general(a, b, (((1,), (1,)), ((), ())), preferred_element_type=F32)


def _swiglu_into(h, wg_ref, wu_ref, act_ref, side_work=()):
    n_chunks = D_FF // FFN_COLS
    after = {((k + 1) * n_chunks) // (len(side_work) + 1) - 1: piece
             for k, piece in enumerate(side_work)}
    for c in range(n_chunks):
        cols = slice(c * FFN_COLS, (c + 1) * FFN_COLS)
        gate = _dot(h[...], wg_ref[:, cols])
        up = _dot(h[...], wu_ref[:, cols])
        act_ref[:, cols] = (gate * jax.nn.sigmoid(gate) * up).astype(BF16)
        if c in after:
            after[c]()


class _CastAhead:
    def __init__(self, steps, stacked_f32, layer):
        self.args = list(stacked_f32)

        def slabs(total_rows, cols):
            n = 1
            while n * 2 <= steps and total_rows % (n * 2 * V7X_BF16_ROWS) == 0:
                n *= 2
            return total_rows // n, cols, n

        self.in_specs, self.out_specs, self.out_shapes = [], [], []
        for rows, cols, n_blocks in (slabs(D_MODEL, D_FF), slabs(D_MODEL, D_FF),
                                     slabs(D_FF, D_MODEL)):
            last = n_blocks - 1
            self.in_specs.append(pl.BlockSpec(
                (None, rows, cols), lambda i, last=last: (layer, jnp.minimum(i, last), 0)))
            self.out_specs.append(pl.BlockSpec(
                (rows, cols), lambda i, last=last: (jnp.minimum(i, last), 0)))
            self.out_shapes.append(jax.ShapeDtypeStruct((rows * n_blocks, cols), BF16))

    @staticmethod
    def run(src_refs, dst_refs):
        for src, dst in zip(src_refs, dst_refs):
            dst[...] = src[...].astype(BF16)


def _ffn_body(*refs, final_norm, pre_proj, cast_ahead):
    refs = list(refs)
    act_ref = refs.pop()
    side_work = []
    if cast_ahead:
        dst_refs = [refs.pop() for _ in range(3)][::-1]
        o_ref = refs.pop()
        src_refs = [refs.pop() for _ in range(3)][::-1]
        side_work.append(functools.partial(_CastAhead.run, src_refs, dst_refs))
    else:
        o_ref = refs.pop()
    x_ref = refs.pop(0)
    x = x_ref[...]
    if pre_proj:
        a_ref, wp_ref = refs.pop(0), refs.pop(0)
        x = x + _dot(a_ref[...], wp_ref[...])
    g_ref, wg_ref, wu_ref, wd_ref = refs[:4]
    _swiglu_into(_rms(x, g_ref[...]).astype(BF16), wg_ref, wu_ref, act_ref, side_work)
    out = x + 0.5 * _dot(act_ref[...], wd_ref[...])
    if final_norm:
        out = _rms(out, refs[4][...])
    o_ref[...] = out


def _ffn(x2d, gain, wg, wu, wd, final_gain=None, proj=None, next_weights=None):
    n = x2d.shape[0]
    steps = n // FFN_ROWS
    row_spec = pl.BlockSpec((FFN_ROWS, D_MODEL), lambda i: (i, 0))
    in_specs, args = [row_spec], [x2d]
    if proj is not None:
        a, w_proj = proj
        in_specs += [pl.BlockSpec((FFN_ROWS, a.shape[1]), lambda i: (i, 0)),
                     _resident(w_proj.shape)]
        args += [a, w_proj]
    in_specs += [_resident(gain.shape), _resident(wg.shape), _resident(wu.shape),
                 _resident(wd.shape)]
    args += [gain, wg, wu, wd]
    if final_gain is not None:
        in_specs.append(_resident((1, D_MODEL)))
        args.append(final_gain.reshape(1, D_MODEL))
    out_specs, out_shapes = [row_spec], [jax.ShapeDtypeStruct(x2d.shape, F32)]
    if next_weights is not None:
        cast = _CastAhead(steps, *next_weights)
        in_specs += cast.in_specs
        args += cast.args
        out_specs += cast.out_specs
        out_shapes += cast.out_shapes
    name = "ffn" + ("_proj" if proj is not None else "") + ("_final" if final_gain is not None else "")
    outs = pl.pallas_call(
        functools.partial(_ffn_body, final_norm=final_gain is not None,
                          pre_proj=proj is not None, cast_ahead=next_weights is not None),
        out_shape=tuple(out_shapes),
        grid=(steps,),
        in_specs=in_specs,
        out_specs=tuple(out_specs),
        scratch_shapes=[pltpu.VMEM((FFN_ROWS, D_FF), BF16)],
        compiler_params=_params("arbitrary"),
        name=name,
    )(*args)
    return outs[0], tuple(outs[1:])


def _pool_prepare(xm_ref, xp_ref, xn_ref, g_ref, ext_ref, *, i, n_i):
    gain = g_ref[...]
    fill = jnp.zeros((POOL_PAD - POOL_HALO, D_MODEL), F32)
    hp = jnp.where(i > 0, _rms(xp_ref[0], gain), 0.0)
    hn = jnp.where(i < n_i - 1, _rms(xn_ref[0], gain), 0.0)
    ext_ref[0:POOL_PAD] = jnp.concatenate([fill, hp], axis=0)
    ext_ref[POOL_PAD:POOL_PAD + POOL_ROWS] = _rms(xm_ref[0], gain)
    ext_ref[POOL_PAD + POOL_ROWS:] = jnp.concatenate([hn, fill], axis=0)


def _pool_group(g, xm_ref, pw_ref, pb_ref, ps_ref, o_ref, ext_ref, *, seq, i):
    w = POOL_WINDOWS[g]
    cols = slice(g * POOL_GROUP, (g + 1) * POOL_GROUP)
    wsum = None
    for k in range(-(w // 2), w // 2):
        shifted = ext_ref[POOL_PAD + k:POOL_PAD + k + POOL_ROWS, cols]
        wsum = shifted if wsum is None else wsum + shifted
    t = i * POOL_ROWS + lax.broadcasted_iota(jnp.int32, (POOL_ROWS, 1), 0)
    cnt = jnp.minimum(t + w // 2, seq) - jnp.maximum(t - w // 2, 0)
    hm = ext_ref[POOL_PAD:POOL_PAD + POOL_ROWS, cols]
    pooled = wsum / cnt.astype(F32) - hm
    y = _dot(pooled.astype(BF16), pw_ref[g]) + pb_ref[g]
    o_ref[:, cols] = xm_ref[0, :, cols] + y * ps_ref[:, cols]


def _pool_ffn_body(xm_ref, xp_ref, xn_ref, mg_ref, pw_ref, pb_ref, ps_ref,
                   g_ref, wg_ref, wu_ref, wd_ref, *rest, seq, final_norm, cast_ahead):
    rest = list(rest)
    fg_ref = rest.pop(0) if final_norm else None
    ext_ref, mid_ref, h_ref, act_ref = rest[-4:]
    side_work = []
    if cast_ahead:
        src_refs, o_ref, dst_refs = rest[0:3], rest[3], rest[4:7]
        side_work.append(functools.partial(_CastAhead.run, src_refs, dst_refs))
    else:
        o_ref = rest[0]
    j = pl.program_id(0)
    n_tiles = pl.num_programs(0) - 1
    tiles_per_seq = seq // POOL_ROWS

    @pl.when(j == 0)
    def _():
        mid_ref[...] = jnp.zeros_like(mid_ref)

    x = mid_ref[...]
    o_ref[...] = x
    h_ref[...] = _rms(x, g_ref[...]).astype(BF16)
    i = jnp.minimum(j, n_tiles - 1) % tiles_per_seq
    pool_pieces = [functools.partial(_pool_prepare, xm_ref, xp_ref, xn_ref, mg_ref, ext_ref,
                                     i=i, n_i=tiles_per_seq)]
    pool_pieces += [functools.partial(_pool_group, g, xm_ref, pw_ref, pb_ref, ps_ref,
                                      mid_ref, ext_ref, seq=seq, i=i)
                    for g in range(len(POOL_WINDOWS))]
    _swiglu_into(h_ref, wg_ref, wu_ref, act_ref, side_work=pool_pieces + side_work)
    out = o_ref[...] + 0.5 * _dot(act_ref[...], wd_ref[...])
    if final_norm:
        out = _rms(out, fg_ref[...])
    o_ref[...] = out


def _pool_ffn(x, mixer_gain, pw, pb, ps, gain, wg, wu, wd, final_gain=None, next_weights=None):
    b, s, _ = x.shape
    tiles_per_seq = s // POOL_ROWS
    n_tiles = b * tiles_per_seq
    blocks_per_tile = POOL_ROWS // POOL_HALO
    n_halo_blocks = s // POOL_HALO
    n_groups = len(POOL_WINDOWS)

    def tile(j):
        jp = jnp.minimum(j, n_tiles - 1)
        return jp // tiles_per_seq, jp % tiles_per_seq

    def main_map(j):
        bi, i = tile(j)
        return bi, i, 0

    def prev_map(j):
        bi, i = tile(j)
        return bi, jnp.maximum(i * blocks_per_tile - 1, 0), 0

    def next_map(j):
        bi, i = tile(j)
        return bi, jnp.minimum((i + 1) * blocks_per_tile, n_halo_blocks - 1), 0

    in_specs = [
        pl.BlockSpec((1, POOL_ROWS, D_MODEL), main_map),
        pl.BlockSpec((1, POOL_HALO, D_MODEL), prev_map),
        pl.BlockSpec((1, POOL_HALO, D_MODEL), next_map),
        _resident((1, D_MODEL)),
        _resident((n_groups, POOL_GROUP, POOL_GROUP)),
        _resident((n_groups, 1, POOL_GROUP)),
        _resident((1, D_MODEL)),
        _resident(gain.shape), _resident(wg.shape), _resident(wu.shape), _resident(wd.shape),
    ]
    args = [x, x, x, mixer_gain.reshape(1, D_MODEL), pw.astype(BF16),
            pb.reshape(n_groups, 1, POOL_GROUP), ps.reshape(1, D_MODEL), gain, wg, wu, wd]
    if final_gain is not None:
        in_specs.append(_resident((1, D_MODEL)))
        args.append(final_gain.reshape(1, D_MODEL))
    out_specs = [pl.BlockSpec((POOL_ROWS, D_MODEL), lambda j: (jnp.maximum(j - 1, 0), 0))]
    out_shapes = [jax.ShapeDtypeStruct((b * s, D_MODEL), F32)]
    if next_weights is not None:
        cast = _CastAhead(n_tiles + 1, *next_weights)
        in_specs += cast.in_specs
        args += cast.args
        out_specs += cast.out_specs
        out_shapes += cast.out_shapes
    outs = pl.pallas_call(
        functools.partial(_pool_ffn_body, seq=s, final_norm=final_gain is not None,
                          cast_ahead=next_weights is not None),
        out_shape=tuple(out_shapes),
        grid=(n_tiles + 1,),
        in_specs=in_specs,
        out_specs=tuple(out_specs),
        scratch_shapes=[pltpu.VMEM((POOL_ROWS + 2 * POOL_PAD, D_MODEL), F32),
                        pltpu.VMEM((POOL_ROWS, D_MODEL), F32),
                        pltpu.VMEM((POOL_ROWS, D_MODEL), BF16),
                        pltpu.VMEM((POOL_ROWS, D_FF), BF16)],
        compiler_params=_params("arbitrary"),
        name="pool_ffn_final" if final_gain is not None else "pool_ffn",
    )(*args)
    return outs[0].reshape(b, s, D_MODEL), tuple(outs[1:])


def _dft_tables(n_rows, n_cols, n):
    k = (jnp.arange(n_rows, dtype=jnp.int32)[:, None]
         * jnp.arange(n_cols, dtype=jnp.int32)[None, :]) % n
    ang = k.astype(F32) * (2.0 * math.pi / n)
    return jnp.cos(ang), jnp.sin(ang)


def _seq_dft_matrix(s):
    quarter = s // DFT_RADIX
    root = int(math.isqrt(quarter))
    assert root * root == quarter
    j = jnp.arange(DFT_RADIX * 2 * quarter, dtype=jnp.int32)
    r, part, t = j // (2 * quarter), (j // quarter) % 2, j % quarter
    beta = ((t * r) % s).astype(F32) * (2.0 * math.pi / s) - part.astype(F32) * (math.pi / 2)
    rows = jnp.arange(root, dtype=jnp.int32)[:, None]
    ang_a = ((rows * t[None, :]) % root).astype(F32) * (2.0 * math.pi / root)
    ang_b = ((rows * t[None, :]) % quarter).astype(F32) * (2.0 * math.pi / quarter) + beta[None, :]
    w = (jnp.cos(ang_a)[:, None, :] * jnp.cos(ang_b)[None, :, :]
         - jnp.sin(ang_a)[:, None, :] * jnp.sin(ang_b)[None, :, :])
    return w.astype(BF16).reshape(quarter, DFT_RADIX * 2 * quarter)


def _chan_dft_body(x_ref, g_ref, cc_ref, nsc_ref, u_ref):
    gain = g_ref[...]
    z_re, z_im = [], []
    for q in range(DFT_RADIX):
        h = _rms(x_ref[0, q], gain).astype(BF16)
        re, im = [], []
        for g in range(N_FOURIER_GROUPS):
            cols = slice(g * FOURIER_GROUP, (g + 1) * FOURIER_GROUP)
            re.append(_dot(h[:, cols], cc_ref[...]))
            im.append(_dot(h[:, cols], nsc_ref[...]))
        z_re.append(jnp.concatenate(re, axis=-1))
        z_im.append(jnp.concatenate(im, axis=-1))
    t0 = (z_re[0] + z_re[2], z_im[0] + z_im[2])
    t1 = (z_re[0] - z_re[2], z_im[0] - z_im[2])
    t2 = (z_re[1] + z_re[3], z_im[1] + z_im[3])
    t3 = (z_re[1] - z_re[3], z_im[1] - z_im[3])
    u = [(t0[0] + t2[0], t0[1] + t2[1]),
         (t1[0] + t3[1], t1[1] - t3[0]),
         (t0[0] - t2[0], t0[1] - t2[1]),
         (t1[0] - t3[1], t1[1] + t3[0])]
    for r in range(DFT_RADIX):
        u_ref[0, r, 0] = u[r][0].astype(BF16)
        u_ref[0, r, 1] = u[r][1].astype(BF16)


def _interleave_matrix():
    out_row = jnp.arange(DFT_RADIX * SEQ_DFT_GROUP)[:, None]
    in_row = jnp.arange(DFT_RADIX * SEQ_DFT_GROUP)[None, :]
    src = (out_row % DFT_RADIX) * SEQ_DFT_GROUP + out_row // DFT_RADIX
    return (in_row == src).astype(BF16)


def _seq_dft_body(w_ref, u_ref, x_ref, wo_ref, bo_ref, perm_ref, o_ref, *, inv_norm):
    rows, two_q = w_ref.shape[0], u_ref.shape[2]
    f = [(_dot(w_ref[:, r * two_q:(r + 1) * two_q], u_ref[0, r]) * inv_norm).astype(BF16)
         for r in range(DFT_RADIX)]
    span = DFT_RADIX * SEQ_DFT_GROUP
    for j in range(rows // SEQ_DFT_GROUP):
        grp = slice(j * SEQ_DFT_GROUP, (j + 1) * SEQ_DFT_GROUP)
        stacked = jnp.concatenate([f[r][grp] for r in range(DFT_RADIX)], axis=0)
        f_tok = _dot(perm_ref[...], stacked).astype(BF16)
        tok = slice(j * span, (j + 1) * span)
        o_ref[0, tok, :] = x_ref[0, tok, :] + _dot(f_tok, wo_ref[...]) + bo_ref[...]


def _fourier_mixer(x, gain, w_out, b_out):
    b, s, _ = x.shape
    quarter = s // DFT_RADIX
    cc, sc = _dft_tables(FOURIER_GROUP, FOURIER_GROUP, FOURIER_GROUP)
    u = pl.pallas_call(
        _chan_dft_body,
        out_shape=jax.ShapeDtypeStruct((b, DFT_RADIX, 2, quarter, D_MODEL), BF16),
        grid=(b, quarter // CHAN_DFT_ROWS),
        in_specs=[pl.BlockSpec((1, DFT_RADIX, CHAN_DFT_ROWS, D_MODEL),
                               lambda bi, i: (bi, 0, i, 0)),
                  _resident((1, D_MODEL)),
                  _resident((FOURIER_GROUP, FOURIER_GROUP)),
                  _resident((FOURIER_GROUP, FOURIER_GROUP))],
        out_specs=pl.BlockSpec((1, DFT_RADIX, 2, CHAN_DFT_ROWS, D_MODEL),
                               lambda bi, i: (bi, 0, 0, i, 0)),
        compiler_params=_params("parallel", "parallel"),
        name="fourier_chan_dft",
    )(x.reshape(b, DFT_RADIX, quarter, D_MODEL), gain.reshape(1, D_MODEL),
      cc.astype(BF16), (-sc).astype(BF16))
    u = u.reshape(b, DFT_RADIX, 2 * quarter, D_MODEL)
    inv_norm = 1.0 / math.sqrt(s * FOURIER_GROUP)
    tile = pl.BlockSpec((1, DFT_RADIX * SEQ_DFT_ROWS, D_MODEL), lambda bi, i: (bi, i, 0))
    return pl.pallas_call(
        functools.partial(_seq_dft_body, inv_norm=inv_norm),
        out_shape=jax.ShapeDtypeStruct(x.shape, F32),
        grid=(b, quarter // SEQ_DFT_ROWS),
        in_specs=[pl.BlockSpec((SEQ_DFT_ROWS, DFT_RADIX * 2 * quarter), lambda bi, i: (i, 0)),
                  pl.BlockSpec((1, DFT_RADIX, 2 * quarter, D_MODEL),
                               lambda bi, i: (bi, 0, 0, 0), pipeline_mode=pl.Buffered(1)),
                  tile,
                  _resident((D_MODEL, D_MODEL)),
                  _resident((1, D_MODEL)),
                  _resident((DFT_RADIX * SEQ_DFT_GROUP, DFT_RADIX * SEQ_DFT_GROUP))],
        out_specs=tile,
        compiler_params=_params("parallel", "arbitrary"),
        name="fourier_seq_dft",
    )(_seq_dft_matrix(s), u, x, w_out.astype(BF16), b_out.reshape(1, D_MODEL),
      _interleave_matrix())


def _rope_angles(s):
    rows = s // GRID_W
    row = jnp.repeat(jnp.arange(rows, dtype=F32), GRID_W)
    col = jnp.tile(jnp.arange(GRID_W, dtype=F32), rows)
    half = HEAD_DIM // 2
    inv_freq = ROPE_THETA ** (-jnp.arange(0, half, 2, dtype=F32) / half)
    ang_r = row[:, None] * inv_freq[None, :]
    ang_c = col[:, None] * inv_freq[None, :]
    return jnp.concatenate([ang_r, ang_r, ang_c, ang_c], axis=-1)


def _swap_quarters(a, axis):
    q0, q1, q2, q3 = jnp.split(a, 4, axis=axis)
    return jnp.concatenate([q1, q0, q3, q2], axis=axis)


def _qkv_body(x_ref, g_ref, wqv_ref, wk_ref, kg_ref, qcos_ref, qsin_ref,
              kcos_ref, ksup_ref, ksdn_ref, qt_ref, k_ref, vt_ref):
    h = _rms(x_ref[...], g_ref[...]).astype(BF16)
    rows = h.shape[0]
    k2 = _dot(h, wk_ref[...])
    kcos, ksup, ksdn = kcos_ref[...], ksup_ref[...], ksdn_ref[...]
    quarter = HEAD_DIM // 4
    for g in range(N_KV_HEADS):
        u = k2[:, g * HEAD_DIM:(g + 1) * HEAD_DIM]
        ms = jnp.mean(u * u, axis=-1, keepdims=True)
        u = u * lax.rsqrt(ms + NORM_EPS) * kg_ref[...]
        up = pltpu.roll(u, HEAD_DIM - quarter, 1)
        dn = pltpu.roll(u, quarter, 1)
        k_ref[:, g * HEAD_DIM:(g + 1) * HEAD_DIM] = (
            u * kcos + (up * ksup + dn * ksdn)).astype(BF16)
    pair = 2 * HEAD_DIM
    pair_t = lambda p: _dot_bt(wqv_ref[p * pair:(p + 1) * pair, :], h)
    qcos, qsin = qcos_ref[...], qsin_ref[...]
    ut_next = pair_t(0)
    for hd in range(N_Q_HEADS):
        if hd % 2 == 0:
            ut_pair, ut_next = ut_next, pair_t(hd // 2 + 1)
        u = ut_pair[(hd % 2) * HEAD_DIM:(hd % 2 + 1) * HEAD_DIM, :]
        ms = jnp.mean(u * u, axis=0, keepdims=True)
        n = u * lax.rsqrt(ms + NORM_EPS)
        r = n * qcos + _swap_quarters(n, 0) * qsin
        qt_ref[hd * HEAD_DIM:(hd + 1) * HEAD_DIM, :] = r.astype(BF16)
    ones_row = lax.broadcasted_iota(jnp.int32, (V_AUG_ROWS - HEAD_DIM, rows), 0) == 0
    for g in range(N_KV_HEADS):
        v_t = ut_next[g * HEAD_DIM:(g + 1) * HEAD_DIM, :]
        vt_ref[g, 0:HEAD_DIM, :] = v_t.astype(BF16)
        vt_ref[g, HEAD_DIM:, :] = jnp.where(ones_row, 1.0, 0.0).astype(BF16)


def _attn_body(qt_ref, k_ref, vt_ref, o_ref, s_ref, *, seq):
    n_chunks = seq // ATTN_KV_ROWS
    groups = range(ATTN_COL_GROUPS)
    width = Q_PER_KV * ATTN_Q_ROWS // ATTN_COL_GROUPS

    def scores(c, g, q):
        st = _dot(k_ref[c * ATTN_KV_ROWS:(c + 1) * ATTN_KV_ROWS, :], q)
        s_ref[c % 2, :, g * width:(g + 1) * width] = st
        return jnp.max(st, axis=0, keepdims=True)

    def q_block(i, carry):
        r0 = pl.multiple_of(i * ATTN_Q_ROWS, ATTN_Q_ROWS)
        qb = qt_ref[:, pl.ds(r0, ATTN_Q_ROWS)]
        qcat = jnp.concatenate([qb[r * HEAD_DIM:(r + 1) * HEAD_DIM, :]
                                for r in range(Q_PER_KV)], axis=1)
        q = [qcat[:, g * width:(g + 1) * width] for g in groups]
        m = [None] * ATTN_COL_GROUPS
        acc = [None] * ATTN_COL_GROUPS
        cm = [scores(0, g, q[g]) for g in groups]
        for c in range(n_chunks):
            kv = slice(c * ATTN_KV_ROWS, (c + 1) * ATTN_KV_ROWS)
            for g in groups:
                cm_next = scores(c + 1, g, q[g]) if c + 1 < n_chunks else None
                m_new = cm[g] if m[g] is None else jnp.maximum(m[g], cm[g])
                pv = None
                for kt in range(ATTN_KV_ROWS // V7X_MXU_DIM):
                    rows = slice(kt * V7X_MXU_DIM, (kt + 1) * V7X_MXU_DIM)
                    p = jnp.exp2(s_ref[c % 2, rows, g * width:(g + 1) * width] - m_new).astype(BF16)
                    k0 = c * ATTN_KV_ROWS + kt * V7X_MXU_DIM
                    part = _dot(vt_ref[0, :, k0:k0 + V7X_MXU_DIM], p)
                    pv = part if pv is None else pv + part
                acc[g] = pv if m[g] is None else acc[g] * jnp.exp2(m[g] - m_new) + pv
                m[g], cm[g] = m_new, cm_next
        out_t = jnp.concatenate([a[0:HEAD_DIM] / a[HEAD_DIM:HEAD_DIM + 1] for a in acc], axis=1)
        for r in range(Q_PER_KV):
            head = out_t[:, r * ATTN_Q_ROWS:(r + 1) * ATTN_Q_ROWS].T
            o_ref[0, pl.ds(r0, ATTN_Q_ROWS), r * HEAD_DIM:(r + 1) * HEAD_DIM] = head.astype(BF16)
        return carry

    lax.fori_loop(0, seq // ATTN_Q_ROWS, q_block, 0)


def _attention_mixer(x, gain, w_qkv, q_gain, k_gain, w_o):
    b, s, _ = x.shape
    n = b * s
    x2d = x.reshape(n, D_MODEL)
    ang = _rope_angles(s)
    cos, sin = jnp.cos(ang), jnp.sin(ang)
    first = ((jnp.arange(HEAD_DIM) % (HEAD_DIM // 2)) < HEAD_DIM // 4)
    ksup = jnp.where(first[None, :], -sin, 0.0)
    ksdn = jnp.where(first[None, :], 0.0, sin)
    c = HEAD_DIM ** -0.5 * math.log2(math.e)
    sign = jnp.where(first, -1.0, 1.0)
    qcos = (q_gain * c)[:, None] * cos.T
    qsin = (_swap_quarters(q_gain, 0) * sign * c)[:, None] * sin.T
    wqv_t = jnp.concatenate([w_qkv[:, :D_Q], w_qkv[:, D_Q + D_KV:]], axis=1).T.astype(BF16)
    wk = w_qkv[:, D_Q:D_Q + D_KV].astype(BF16)
    tiles_per_seq = s // TOKEN_ROWS
    row = lambda width: pl.BlockSpec((TOKEN_ROWS, width), lambda i: (i, 0))
    pos = pl.BlockSpec((TOKEN_ROWS, HEAD_DIM), lambda i: (i % tiles_per_seq, 0))
    pos_t = pl.BlockSpec((HEAD_DIM, TOKEN_ROWS), lambda i: (0, i % tiles_per_seq))
    qt, k, vt = pl.pallas_call(
        _qkv_body,
        out_shape=(jax.ShapeDtypeStruct((D_Q, n), BF16),
                   jax.ShapeDtypeStruct((n, D_KV), BF16),
                   jax.ShapeDtypeStruct((N_KV_HEADS, V_AUG_ROWS, n), BF16)),
        grid=(n // TOKEN_ROWS,),
        in_specs=[row(D_MODEL), _resident((1, D_MODEL)),
                  _resident((D_Q + D_KV, D_MODEL)), _resident((D_MODEL, D_KV)),
                  _resident((1, HEAD_DIM)), pos_t, pos_t, pos, pos, pos],
        out_specs=(pl.BlockSpec((D_Q, TOKEN_ROWS), lambda i: (0, i)), row(D_KV),
                   pl.BlockSpec((N_KV_HEADS, V_AUG_ROWS, TOKEN_ROWS), lambda i: (0, 0, i))),
        compiler_params=_params("parallel"),
        name="attn_qkv",
    )(x2d, gain.reshape(1, D_MODEL), wqv_t, wk, k_gain.reshape(1, HEAD_DIM),
      qcos, qsin, cos, ksup, ksdn)

    group_w = Q_PER_KV * HEAD_DIM
    attn = pl.pallas_call(
        functools.partial(_attn_body, seq=s),
        out_shape=jax.ShapeDtypeStruct((b, s, D_Q), BF16),
        grid=(b, N_KV_HEADS),
        in_specs=[pl.BlockSpec((group_w, s), lambda bi, g: (g, bi)),
                  pl.BlockSpec((s, HEAD_DIM), lambda bi, g: (bi, g)),
                  pl.BlockSpec((1, V_AUG_ROWS, s), lambda bi, g: (g, 0, bi))],
        out_specs=pl.BlockSpec((1, s, group_w), lambda bi, g: (bi, 0, g)),
        scratch_shapes=[pltpu.VMEM((2, ATTN_KV_ROWS, Q_PER_KV * ATTN_Q_ROWS), F32)],
        compiler_params=_params("parallel", "parallel"),
        name="attn_core",
    )(qt, k, vt)
    return attn.reshape(n, D_Q), w_o.astype(BF16)


def kernel(x, ffn1_norm, ffn1_w_gate, ffn1_w_up, ffn1_w_down, mixer_norm, ffn2_norm, ffn2_w_gate, ffn2_w_up, ffn2_w_down, pool_w, pool_b, pool_scale, fourier_w, fourier_b, attn_w_qkv, attn_q_norm, attn_k_norm, attn_w_o, final_norm):
    b, s, d = x.shape
    depth = ffn1_norm.shape[0]
    n = b * s
    ffn1_f32 = (ffn1_w_gate, ffn1_w_up, ffn1_w_down)
    ffn2_f32 = (ffn2_w_gate, ffn2_w_up, ffn2_w_down)
    weights = tuple(w[0].astype(BF16) for w in ffn1_f32)

    def ffn(x3d, gain, weights, **kw):
        out, next_w = _ffn(x3d.reshape(n, d), gain.reshape(1, d), *weights, **kw)
        return out.reshape(b, s, d), next_w

    for i in range(depth):
        x, weights = ffn(x, ffn1_norm[i], weights, next_weights=(ffn2_f32, i))
        kind, j = i % N_MIXERS, i // N_MIXERS
        final_gain = final_norm if i == depth - 1 else None
        following = (ffn1_f32, i + 1) if i + 1 < depth else None
        if kind == 0:
            x, weights = _pool_ffn(x, mixer_norm[i], pool_w[j], pool_b[j], pool_scale[j],
                                   ffn2_norm[i].reshape(1, d), *weights,
                                   final_gain=final_gain, next_weights=following)
            continue
        proj = None
        if kind == 1:
            x = _fourier_mixer(x, mixer_norm[i], fourier_w[j], fourier_b[j])
        else:
            proj = _attention_mixer(x, mixer_norm[i], attn_w_qkv[j], attn_q_norm[j],
                                    attn_k_norm[j], attn_w_o[j])
        x, weights = ffn(x, ffn2_norm[i], weights, final_gain=final_gain, proj=proj,
                         next_weights=following)
    return x
```

```python
import functools
import math

import jax
import jax.numpy as jnp
from jax import lax
from jax.experimental import pallas as pl
from jax.experimental.pallas import tpu as pltpu

D_MODEL = 1024
D_FF = 2816
N_MIXERS = 3
NORM_EPS = 1e-6
POOL_WINDOWS = (2, 4, 8, 16)
POOL_GROUP = D_MODEL // len(POOL_WINDOWS)
POOL_HALO = max(POOL_WINDOWS) // 2
N_FOURIER_GROUPS = 4
FOURIER_GROUP = D_MODEL // N_FOURIER_GROUPS
DFT_RADIX = 4
HEAD_DIM = 128
N_Q_HEADS = D_MODEL // HEAD_DIM
N_KV_HEADS = N_Q_HEADS // 4
Q_PER_KV = N_Q_HEADS // N_KV_HEADS
D_Q = N_Q_HEADS * HEAD_DIM
D_KV = N_KV_HEADS * HEAD_DIM
V_AUG_ROWS = HEAD_DIM + 16
GRID_W = 64
ROPE_THETA = 10000.0

V7X_MXU_DIM = 256
V7X_BF16_ROWS = 16
V7X_VMEM_BYTES = 64 * 1024 * 1024
VMEM_LIMIT_BYTES = V7X_VMEM_BYTES * 7 // 8

FFN_ROWS = 1024
FFN_COLS = V7X_MXU_DIM
POOL_ROWS = 512
POOL_PAD = V7X_BF16_ROWS
TOKEN_ROWS = 512
CHAN_DFT_ROWS = 512
SEQ_DFT_ROWS = V7X_MXU_DIM
SEQ_DFT_GROUP = V7X_MXU_DIM // DFT_RADIX
ATTN_Q_ROWS = 512
ATTN_KV_ROWS = 512
ATTN_COL_GROUPS = 8

F32 = jnp.float32
BF16 = jnp.bfloat16


def _params(*semantics):
    return pltpu.CompilerParams(dimension_semantics=semantics,
                                vmem_limit_bytes=VMEM_LIMIT_BYTES)


def _resident(shape, index=None):
    lead = tuple(index if d is None else 0 for d in shape)
    return pl.BlockSpec(shape, lambda *_: lead, pipeline_mode=pl.Buffered(1))


def _rms(x, gain):
    ms = jnp.mean(x * x, axis=-1, keepdims=True)
    return x * lax.rsqrt(ms + NORM_EPS) * gain


def _dot(a, b):
    return jnp.dot(a, b, preferred_element_type=F32)


def _dot_bt(a, b):
    return lax.dot_general(a, b, (((1,), (1,)), ((), ())), preferred_element_type=F32)


def _swiglu_into(h, wg_ref, wu_ref, act_ref, side_work=()):
    n_chunks = D_FF // FFN_COLS
    after = {((k + 1) * n_chunks) // (len(side_work) + 1) - 1: piece
             for k, piece in enumerate(side_work)}
    for c in range(n_chunks):
        cols = slice(c * FFN_COLS, (c + 1) * FFN_COLS)
        gate = _dot(h[...], wg_ref[:, cols])
        up = _dot(h[...], wu_ref[:, cols])
        act_ref[:, cols] = (gate * jax.nn.sigmoid(gate) * up).astype(BF16)
        if c in after:
            after[c]()


class _CastAhead:
    def __init__(self, steps, stacked_f32, layer):
        self.args = list(stacked_f32)

        def slabs(total_rows, cols):
            n = 1
            while n * 2 <= steps and total_rows % (n * 2 * V7X_BF16_ROWS) == 0:
                n *= 2
            return total_rows // n, cols, n

        self.in_specs, self.out_specs, self.out_shapes = [], [], []
        for rows, cols, n_blocks in (slabs(D_MODEL, D_FF), slabs(D_MODEL, D_FF),
                                     slabs(D_FF, D_MODEL)):
            last = n_blocks - 1
            self.in_specs.append(pl.BlockSpec(
                (None, rows, cols), lambda i, last=last: (layer, jnp.minimum(i, last), 0)))
            self.out_specs.append(pl.BlockSpec(
                (rows, cols), lambda i, last=last: (jnp.minimum(i, last), 0)))
            self.out_shapes.append(jax.ShapeDtypeStruct((rows * n_blocks, cols), BF16))

    @staticmethod
    def run(src_refs, dst_refs):
        for src, dst in zip(src_refs, dst_refs):
            dst[...] = src[...].astype(BF16)


def _ffn_body(*refs, final_norm, pre_proj, cast_ahead):
    refs = list(refs)
    act_ref = refs.pop()
    side_work = []
    if cast_ahead:
        dst_refs = [refs.pop() for _ in range(3)][::-1]
        o_ref = refs.pop()
        src_refs = [refs.pop() for _ in range(3)][::-1]
        side_work.append(functools.partial(_CastAhead.run, src_refs, dst_refs))
    else:
        o_ref = refs.pop()
    x_ref = refs.pop(0)
    x = x_ref[...]
    if pre_proj:
        a_ref, wp_ref = refs.pop(0), refs.pop(0)
        x = x + _dot(a_ref[...], wp_ref[...])
    g_ref, wg_ref, wu_ref, wd_ref = refs[:4]
    _swiglu_into(_rms(x, g_ref[...]).astype(BF16), wg_ref, wu_ref, act_ref, side_work)
    out = x + 0.5 * _dot(act_ref[...], wd_ref[...])
    if final_norm:
        out = _rms(out, refs[4][...])
    o_ref[...] = out


def _ffn(x2d, gain, wg, wu, wd, final_gain=None, proj=None, next_weights=None):
    n = x2d.shape[0]
    steps = n // FFN_ROWS
    row_spec = pl.BlockSpec((FFN_ROWS, D_MODEL), lambda i: (i, 0))
    in_specs, args = [row_spec], [x2d]
    if proj is not None:
        a, w_proj = proj
        in_specs += [pl.BlockSpec((FFN_ROWS, a.shape[1]), lambda i: (i, 0)),
                     _resident(w_proj.shape)]
        args += [a, w_proj]
    in_specs += [_resident(gain.shape), _resident(wg.shape), _resident(wu.shape),
                 _resident(wd.shape)]
    args += [gain, wg, wu, wd]
    if final_gain is not None:
        in_specs.append(_resident((1, D_MODEL)))
        args.append(final_gain.reshape(1, D_MODEL))
    out_specs, out_shapes = [row_spec], [jax.ShapeDtypeStruct(x2d.shape, F32)]
    if next_weights is not None:
        cast = _CastAhead(steps, *next_weights)
        in_specs += cast.in_specs
        args += cast.args
        out_specs += cast.out_specs
        out_shapes += cast.out_shapes
    name = "ffn" + ("_proj" if proj is not None else "") + ("_final" if final_gain is not None else "")
    outs = pl.pallas_call(
        functools.partial(_ffn_body, final_norm=final_gain is not None,
                          pre_proj=proj is not None, cast_ahead=next_weights is not None),
        out_shape=tuple(out_shapes),
        grid=(steps,),
        in_specs=in_specs,
        out_specs=tuple(out_specs),
        scratch_shapes=[pltpu.VMEM((FFN_ROWS, D_FF), BF16)],
        compiler_params=_params("arbitrary"),
        name=name,
    )(*args)
    return outs[0], tuple(outs[1:])


def _pool_prepare(xm_ref, xp_ref, xn_ref, g_ref, ext_ref, *, i, n_i):
    gain = g_ref[...]
    fill = jnp.zeros((POOL_PAD - POOL_HALO, D_MODEL), F32)
    hp = jnp.where(i > 0, _rms(xp_ref[0], gain), 0.0)
    hn = jnp.where(i < n_i - 1, _rms(xn_ref[0], gain), 0.0)
    ext_ref[0:POOL_PAD] = jnp.concatenate([fill, hp], axis=0)
    ext_ref[POOL_PAD:POOL_PAD + POOL_ROWS] = _rms(xm_ref[0], gain)
    ext_ref[POOL_PAD + POOL_ROWS:] = jnp.concatenate([hn, fill], axis=0)


def _pool_group(g, xm_ref, pw_ref, pb_ref, ps_ref, o_ref, ext_ref, *, seq, i):
    w = POOL_WINDOWS[g]
    cols = slice(g * POOL_GROUP, (g + 1) * POOL_GROUP)
    wsum = None
    for k in range(-(w // 2), w // 2):
        shifted = ext_ref[POOL_PAD + k:POOL_PAD + k + POOL_ROWS, cols]
        wsum = shifted if wsum is None else wsum + shifted
    t = i * POOL_ROWS + lax.broadcasted_iota(jnp.int32, (POOL_ROWS, 1), 0)
    cnt = jnp.minimum(t + w // 2, seq) - jnp.maximum(t - w // 2, 0)
    hm = ext_ref[POOL_PAD:POOL_PAD + POOL_ROWS, cols]
    pooled = wsum / cnt.astype(F32) - hm
    y = _dot(pooled.astype(BF16), pw_ref[g]) + pb_ref[g]
    o_ref[:, cols] = xm_ref[0, :, cols] + y * ps_ref[:, cols]


def _pool_ffn_body(xm_ref, xp_ref, xn_ref, mg_ref, pw_ref, pb_ref, ps_ref,
                   g_ref, wg_ref, wu_ref, wd_ref, *rest, seq, final_norm, cast_ahead):
    rest = list(rest)
    fg_ref = rest.pop(0) if final_norm else None
    ext_ref, mid_ref, h_ref, act_ref = rest[-4:]
    side_work = []
    if cast_ahead:
        src_refs, o_ref, dst_refs = rest[0:3], rest[3], rest[4:7]
        side_work.append(functools.partial(_CastAhead.run, src_refs, dst_refs))
    else:
        o_ref = rest[0]
    j = pl.program_id(0)
    n_tiles = pl.num_programs(0) - 1
    tiles_per_seq = seq // POOL_ROWS

    @pl.when(j == 0)
    def _():
        mid_ref[...] = jnp.zeros_like(mid_ref)
        h_ref[...] = jnp.zeros_like(h_ref)

    o_ref[...] = mid_ref[...]
    i = jnp.minimum(j, n_tiles - 1) % tiles_per_seq
    pool_pieces = [functools.partial(_pool_prepare, xm_ref, xp_ref, xn_ref, mg_ref, ext_ref,
                                     i=i, n_i=tiles_per_seq)]
    pool_pieces += [functools.partial(_pool_group, g, xm_ref, pw_ref, pb_ref, ps_ref,
                                      mid_ref, ext_ref, seq=seq, i=i)
                    for g in range(len(POOL_WINDOWS))]
    _swiglu_into(h_ref, wg_ref, wu_ref, act_ref, side_work=pool_pieces + side_work)
    h_ref[...] = _rms(mid_ref[...], g_ref[...]).astype(BF16)
    out = o_ref[...] + 0.5 * _dot(act_ref[...], wd_ref[...])
    if final_norm:
        out = _rms(out, fg_ref[...])
    o_ref[...] = out


def _pool_ffn(x, mixer_gain, pw, pb, ps, gain, wg, wu, wd, final_gain=None, next_weights=None):
    b, s, _ = x.shape
    tiles_per_seq = s // POOL_ROWS
    n_tiles = b * tiles_per_seq
    blocks_per_tile = POOL_ROWS // POOL_HALO
    n_halo_blocks = s // POOL_HALO
    n_groups = len(POOL_WINDOWS)

    def tile(j):
        jp = jnp.minimum(j, n_tiles - 1)
        return jp // tiles_per_seq, jp % tiles_per_seq

    def main_map(j):
        bi, i = tile(j)
        return bi, i, 0

    def prev_map(j):
        bi, i = tile(j)
        return bi, jnp.maximum(i * blocks_per_tile - 1, 0), 0

    def next_map(j):
        bi, i = tile(j)
        return bi, jnp.minimum((i + 1) * blocks_per_tile, n_halo_blocks - 1), 0

    in_specs = [
        pl.BlockSpec((1, POOL_ROWS, D_MODEL), main_map),
        pl.BlockSpec((1, POOL_HALO, D_MODEL), prev_map),
        pl.BlockSpec((1, POOL_HALO, D_MODEL), next_map),
        _resident((1, D_MODEL)),
        _resident((n_groups, POOL_GROUP, POOL_GROUP)),
        _resident((n_groups, 1, POOL_GROUP)),
        _resident((1, D_MODEL)),
        _resident(gain.shape), _resident(wg.shape), _resident(wu.shape), _resident(wd.shape),
    ]
    args = [x, x, x, mixer_gain.reshape(1, D_MODEL), pw.astype(BF16),
            pb.reshape(n_groups, 1, POOL_GROUP), ps.reshape(1, D_MODEL), gain, wg, wu, wd]
    if final_gain is not None:
        in_specs.append(_resident((1, D_MODEL)))
        args.append(final_gain.reshape(1, D_MODEL))
    out_specs = [pl.BlockSpec((POOL_ROWS, D_MODEL), lambda j: (jnp.maximum(j - 1, 0), 0))]
    out_shapes = [jax.ShapeDtypeStruct((b * s, D_MODEL), F32)]
    if next_weights is not None:
        cast = _CastAhead(n_tiles + 1, *next_weights)
        in_specs += cast.in_specs
        args += cast.args
        out_specs += cast.out_specs
        out_shapes += cast.out_shapes
    outs = pl.pallas_call(
        functools.partial(_pool_ffn_body, seq=s, final_norm=final_gain is not None,
                          cast_ahead=next_weights is not None),
        out_shape=tuple(out_shapes),
        grid=(n_tiles + 1,),
        in_specs=in_specs,
        out_specs=tuple(out_specs),
        scratch_shapes=[pltpu.VMEM((POOL_ROWS + 2 * POOL_PAD, D_MODEL), F32),
                        pltpu.VMEM((POOL_ROWS, D_MODEL), F32),
                        pltpu.VMEM((POOL_ROWS, D_MODEL), BF16),
                        pltpu.VMEM((POOL_ROWS, D_FF), BF16)],
        compiler_params=_params("arbitrary"),
        name="pool_ffn_final" if final_gain is not None else "pool_ffn",
    )(*args)
    return outs[0].reshape(b, s, D_MODEL), tuple(outs[1:])


def _dft_tables(n_rows, n_cols, n):
    k = (jnp.arange(n_rows, dtype=jnp.int32)[:, None]
         * jnp.arange(n_cols, dtype=jnp.int32)[None, :]) % n
    ang = k.astype(F32) * (2.0 * math.pi / n)
    return jnp.cos(ang), jnp.sin(ang)


def _seq_dft_matrix(s):
    quarter = s // DFT_RADIX
    root = int(math.isqrt(quarter))
    assert root * root == quarter
    j = jnp.arange(DFT_RADIX * 2 * quarter, dtype=jnp.int32)
    r, part, t = j // (2 * quarter), (j // quarter) % 2, j % quarter
    beta = ((t * r) % s).astype(F32) * (2.0 * math.pi / s) - part.astype(F32) * (math.pi / 2)
    rows = jnp.arange(root, dtype=jnp.int32)[:, None]
    ang_a = ((rows * t[None, :]) % root).astype(F32) * (2.0 * math.pi / root)
    ang_b = ((rows * t[None, :]) % quarter).astype(F32) * (2.0 * math.pi / quarter) + beta[None, :]
    w = (jnp.cos(ang_a)[:, None, :] * jnp.cos(ang_b)[None, :, :]
         - jnp.sin(ang_a)[:, None, :] * jnp.sin(ang_b)[None, :, :])
    return w.astype(BF16).reshape(quarter, DFT_RADIX * 2 * quarter)


def _chan_dft_body(x_ref, g_ref, cc_ref, nsc_ref, u_ref):
    gain = g_ref[...]
    z_re, z_im = [], []
    for q in range(DFT_RADIX):
        h = _rms(x_ref[0, q], gain).astype(BF16)
        re, im = [], []
        for g in range(N_FOURIER_GROUPS):
            cols = slice(g * FOURIER_GROUP, (g + 1) * FOURIER_GROUP)
            re.append(_dot(h[:, cols], cc_ref[...]))
            im.append(_dot(h[:, cols], nsc_ref[...]))
        z_re.append(jnp.concatenate(re, axis=-1))
        z_im.append(jnp.concatenate(im, axis=-1))
    t0 = (z_re[0] + z_re[2], z_im[0] + z_im[2])
    t1 = (z_re[0] - z_re[2], z_im[0] - z_im[2])
    t2 = (z_re[1] + z_re[3], z_im[1] + z_im[3])
    t3 = (z_re[1] - z_re[3], z_im[1] - z_im[3])
    u = [(t0[0] + t2[0], t0[1] + t2[1]),
         (t1[0] + t3[1], t1[1] - t3[0]),
         (t0[0] - t2[0], t0[1] - t2[1]),
         (t1[0] - t3[1], t1[1] + t3[0])]
    for r in range(DFT_RADIX):
        u_ref[0, r, 0] = u[r][0].astype(BF16)
        u_ref[0, r, 1] = u[r][1].astype(BF16)


def _interleave_matrix():
    out_row = jnp.arange(DFT_RADIX * SEQ_DFT_GROUP)[:, None]
    in_row = jnp.arange(DFT_RADIX * SEQ_DFT_GROUP)[None, :]
    src = (out_row % DFT_RADIX) * SEQ_DFT_GROUP + out_row // DFT_RADIX
    return (in_row == src).astype(BF16)


def _seq_dft_body(w_ref, u_ref, x_ref, wo_ref, bo_ref, perm_ref, o_ref, *, inv_norm):
    rows, two_q = w_ref.shape[0], u_ref.shape[2]
    f = [(_dot(w_ref[:, r * two_q:(r + 1) * two_q], u_ref[0, r]) * inv_norm).astype(BF16)
         for r in range(DFT_RADIX)]
    span = DFT_RADIX * SEQ_DFT_GROUP
    for j in range(rows // SEQ_DFT_GROUP):
        grp = slice(j * SEQ_DFT_GROUP, (j + 1) * SEQ_DFT_GROUP)
        stacked = jnp.concatenate([f[r][grp] for r in range(DFT_RADIX)], axis=0)
        f_tok = _dot(perm_ref[...], stacked).astype(BF16)
        tok = slice(j * span, (j + 1) * span)
        o_ref[0, tok, :] = x_ref[0, tok, :] + _dot(f_tok, wo_ref[...]) + bo_ref[...]


def _fourier_mixer(x, gain, w_out, b_out):
    b, s, _ = x.shape
    quarter = s // DFT_RADIX
    cc, sc = _dft_tables(FOURIER_GROUP, FOURIER_GROUP, FOURIER_GROUP)
    u = pl.pallas_call(
        _chan_dft_body,
        out_shape=jax.ShapeDtypeStruct((b, DFT_RADIX, 2, quarter, D_MODEL), BF16),
        grid=(b, quarter // CHAN_DFT_ROWS),
        in_specs=[pl.BlockSpec((1, DFT_RADIX, CHAN_DFT_ROWS, D_MODEL),
                               lambda bi, i: (bi, 0, i, 0)),
                  _resident((1, D_MODEL)),
                  _resident((FOURIER_GROUP, FOURIER_GROUP)),
                  _resident((FOURIER_GROUP, FOURIER_GROUP))],
        out_specs=pl.BlockSpec((1, DFT_RADIX, 2, CHAN_DFT_ROWS, D_MODEL),
                               lambda bi, i: (bi, 0, 0, i, 0)),
        compiler_params=_params("parallel", "parallel"),
        name="fourier_chan_dft",
    )(x.reshape(b, DFT_RADIX, quarter, D_MODEL), gain.reshape(1, D_MODEL),
      cc.astype(BF16), (-sc).astype(BF16))
    u = u.reshape(b, DFT_RADIX, 2 * quarter, D_MODEL)
    inv_norm = 1.0 / math.sqrt(s * FOURIER_GROUP)
    tile = pl.BlockSpec((1, DFT_RADIX * SEQ_DFT_ROWS, D_MODEL), lambda bi, i: (bi, i, 0))
    return pl.pallas_call(
        functools.partial(_seq_dft_body, inv_norm=inv_norm),
        out_shape=jax.ShapeDtypeStruct(x.shape, F32),
        grid=(b, quarter // SEQ_DFT_ROWS),
        in_specs=[pl.BlockSpec((SEQ_DFT_ROWS, DFT_RADIX * 2 * quarter), lambda bi, i: (i, 0)),
                  pl.BlockSpec((1, DFT_RADIX, 2 * quarter, D_MODEL),
                               lambda bi, i: (bi, 0, 0, 0), pipeline_mode=pl.Buffered(1)),
                  tile,
                  _resident((D_MODEL, D_MODEL)),
                  _resident((1, D_MODEL)),
                  _resident((DFT_RADIX * SEQ_DFT_GROUP, DFT_RADIX * SEQ_DFT_GROUP))],
        out_specs=tile,
        compiler_params=_params("parallel", "arbitrary"),
        name="fourier_seq_dft",
    )(_seq_dft_matrix(s), u, x, w_out.astype(BF16), b_out.reshape(1, D_MODEL),
      _interleave_matrix())


def _rope_angles(s):
    rows = s // GRID_W
    row = jnp.repeat(jnp.arange(rows, dtype=F32), GRID_W)
    col = jnp.tile(jnp.arange(GRID_W, dtype=F32), rows)
    half = HEAD_DIM // 2
    inv_freq = ROPE_THETA ** (-jnp.arange(0, half, 2, dtype=F32) / half)
    ang_r = row[:, None] * inv_freq[None, :]
    ang_c = col[:, None] * inv_freq[None, :]
    return jnp.concatenate([ang_r, ang_r, ang_c, ang_c], axis=-1)


def _swap_quarters(a, axis):
    q0, q1, q2, q3 = jnp.split(a, 4, axis=axis)
    return jnp.concatenate([q1, q0, q3, q2], axis=axis)


def _qkv_body(x_ref, g_ref, wqv_ref, wk_ref, kg_ref, qcos_ref, qsin_ref,
              kcos_ref, ksup_ref, ksdn_ref, qt_ref, k_ref, vt_ref):
    h = _rms(x_ref[...], g_ref[...]).astype(BF16)
    rows = h.shape[0]
    k2 = _dot(h, wk_ref[...])
    kcos, ksup, ksdn = kcos_ref[...], ksup_ref[...], ksdn_ref[...]
    quarter = HEAD_DIM // 4
    for g in range(N_KV_HEADS):
        u = k2[:, g * HEAD_DIM:(g + 1) * HEAD_DIM]
        ms = jnp.mean(u * u, axis=-1, keepdims=True)
        u = u * lax.rsqrt(ms + NORM_EPS) * kg_ref[...]
        up = pltpu.roll(u, HEAD_DIM - quarter, 1)
        dn = pltpu.roll(u, quarter, 1)
        k_ref[:, g * HEAD_DIM:(g + 1) * HEAD_DIM] = (
            u * kcos + (up * ksup + dn * ksdn)).astype(BF16)
    pair = 2 * HEAD_DIM
    pair_t = lambda p: _dot_bt(wqv_ref[p * pair:(p + 1) * pair, :], h)
    qcos, qsin = qcos_ref[...], qsin_ref[...]
    ut_next = pair_t(0)
    for hd in range(N_Q_HEADS):
        if hd % 2 == 0:
            ut_pair, ut_next = ut_next, pair_t(hd // 2 + 1)
        u = ut_pair[(hd % 2) * HEAD_DIM:(hd % 2 + 1) * HEAD_DIM, :]
        ms = jnp.mean(u * u, axis=0, keepdims=True)
        n = u * lax.rsqrt(ms + NORM_EPS)
        r = n * qcos + _swap_quarters(n, 0) * qsin
        qt_ref[hd * HEAD_DIM:(hd + 1) * HEAD_DIM, :] = r.astype(BF16)
    ones_row = lax.broadcasted_iota(jnp.int32, (V_AUG_ROWS - HEAD_DIM, rows), 0) == 0
    for g in range(N_KV_HEADS):
        v_t = ut_next[g * HEAD_DIM:(g + 1) * HEAD_DIM, :]
        vt_ref[g, 0:HEAD_DIM, :] = v_t.astype(BF16)
        vt_ref[g, HEAD_DIM:, :] = jnp.where(ones_row, 1.0, 0.0).astype(BF16)


def _attn_body(qt_ref, k_ref, vt_ref, o_ref, s_ref, *, seq):
    n_chunks = seq // ATTN_KV_ROWS
    groups = range(ATTN_COL_GROUPS)
    width = Q_PER_KV * ATTN_Q_ROWS // ATTN_COL_GROUPS

    def scores(c, g, q):
        st = _dot(k_ref[c * ATTN_KV_ROWS:(c + 1) * ATTN_KV_ROWS, :], q)
        s_ref[c % 2, :, g * width:(g + 1) * width] = st
        return jnp.max(st, axis=0, keepdims=True)

    def q_block(i, carry):
        r0 = pl.multiple_of(i * ATTN_Q_ROWS, ATTN_Q_ROWS)
        qb = qt_ref[:, pl.ds(r0, ATTN_Q_ROWS)]
        qcat = jnp.concatenate([qb[r * HEAD_DIM:(r + 1) * HEAD_DIM, :]
                                for r in range(Q_PER_KV)], axis=1)
        q = [qcat[:, g * width:(g + 1) * width] for g in groups]
        m = [None] * ATTN_COL_GROUPS
        acc = [None] * ATTN_COL_GROUPS
        cm = [scores(0, g, q[g]) for g in groups]
        for c in range(n_chunks):
            kv = slice(c * ATTN_KV_ROWS, (c + 1) * ATTN_KV_ROWS)
            for g in groups:
                cm_next = scores(c + 1, g, q[g]) if c + 1 < n_chunks else None
                m_new = cm[g] if m[g] is None else jnp.maximum(m[g], cm[g])
                pv = None
                for kt in range(ATTN_KV_ROWS // V7X_MXU_DIM):
                    rows = slice(kt * V7X_MXU_DIM, (kt + 1) * V7X_MXU_DIM)
                    p = jnp.exp2(s_ref[c % 2, rows, g * width:(g + 1) * width] - m_new).astype(BF16)
                    k0 = c * ATTN_KV_ROWS + kt * V7X_MXU_DIM
                    part = _dot(vt_ref[0, :, k0:k0 + V7X_MXU_DIM], p)
                    pv = part if pv is None else pv + part
                acc[g] = pv if m[g] is None else acc[g] * jnp.exp2(m[g] - m_new) + pv
                m[g], cm[g] = m_new, cm_next
        out_t = jnp.concatenate([a[0:HEAD_DIM] / a[HEAD_DIM:HEAD_DIM + 1] for a in acc], axis=1)
        for r in range(Q_PER_KV):
            head = out_t[:, r * ATTN_Q_ROWS:(r + 1) * ATTN_Q_ROWS].T
            o_ref[0, pl.ds(r0, ATTN_Q_ROWS), r * HEAD_DIM:(r + 1) * HEAD_DIM] = head.astype(BF16)
        return carry

    lax.fori_loop(0, seq // ATTN_Q_ROWS, q_block, 0)


def _attention_mixer(x, gain, w_qkv, q_gain, k_gain, w_o):
    b, s, _ = x.shape
    n = b * s
    x2d = x.reshape(n, D_MODEL)
    ang = _rope_angles(s)
    cos, sin = jnp.cos(ang), jnp.sin(ang)
    first = ((jnp.arange(HEAD_DIM) % (HEAD_DIM // 2)) < HEAD_DIM // 4)
    ksup = jnp.where(first[None, :], -sin, 0.0)
    ksdn = jnp.where(first[None, :], 0.0, sin)
    c = HEAD_DIM ** -0.5 * math.log2(math.e)
    sign = jnp.where(first, -1.0, 1.0)
    qcos = (q_gain * c)[:, None] * cos.T
    qsin = (_swap_quarters(q_gain, 0) * sign * c)[:, None] * sin.T
    wqv_t = jnp.concatenate([w_qkv[:, :D_Q], w_qkv[:, D_Q + D_KV:]], axis=1).T.astype(BF16)
    wk = w_qkv[:, D_Q:D_Q + D_KV].astype(BF16)
    tiles_per_seq = s // TOKEN_ROWS
    row = lambda width: pl.BlockSpec((TOKEN_ROWS, width), lambda i: (i, 0))
    pos = pl.BlockSpec((TOKEN_ROWS, HEAD_DIM), lambda i: (i % tiles_per_seq, 0))
    pos_t = pl.BlockSpec((HEAD_DIM, TOKEN_ROWS), lambda i: (0, i % tiles_per_seq))
    qt, k, vt = pl.pallas_call(
        _qkv_body,
        out_shape=(jax.ShapeDtypeStruct((D_Q, n), BF16),
                   jax.ShapeDtypeStruct((n, D_KV), BF16),
                   jax.ShapeDtypeStruct((N_KV_HEADS, V_AUG_ROWS, n), BF16)),
        grid=(n // TOKEN_ROWS,),
        in_specs=[row(D_MODEL), _resident((1, D_MODEL)),
                  _resident((D_Q + D_KV, D_MODEL)), _resident((D_MODEL, D_KV)),
                  _resident((1, HEAD_DIM)), pos_t, pos_t, pos, pos, pos],
        out_specs=(pl.BlockSpec((D_Q, TOKEN_ROWS), lambda i: (0, i)), row(D_KV),
                   pl.BlockSpec((N_KV_HEADS, V_AUG_ROWS, TOKEN_ROWS), lambda i: (0, 0, i))),
        compiler_params=_params("parallel"),
        name="attn_qkv",
    )(x2d, gain.reshape(1, D_MODEL), wqv_t, wk, k_gain.reshape(1, HEAD_DIM),
      qcos, qsin, cos, ksup, ksdn)

    group_w = Q_PER_KV * HEAD_DIM
    attn = pl.pallas_call(
        functools.partial(_attn_body, seq=s),
        out_shape=jax.ShapeDtypeStruct((b, s, D_Q), BF16),
        grid=(b, N_KV_HEADS),
        in_specs=[pl.BlockSpec((group_w, s), lambda bi, g: (g, bi)),
                  pl.BlockSpec((s, HEAD_DIM), lambda bi, g: (bi, g)),
                  pl.BlockSpec((1, V_AUG_ROWS, s), lambda bi, g: (g, 0, bi))],
        out_specs=pl.BlockSpec((1, s, group_w), lambda bi, g: (bi, 0, g)),
        scratch_shapes=[pltpu.VMEM((2, ATTN_KV_ROWS, Q_PER_KV * ATTN_Q_ROWS), F32)],
        compiler_params=_params("parallel", "parallel"),
        name="attn_core",
    )(qt, k, vt)
    return attn.reshape(n, D_Q), w_o.astype(BF16)


def kernel(x, ffn1_norm, ffn1_w_gate, ffn1_w_up, ffn1_w_down, mixer_norm, ffn2_norm, ffn2_w_gate, ffn2_w_up, ffn2_w_down, pool_w, pool_b, pool_scale, fourier_w, fourier_b, attn_w_qkv, attn_q_norm, attn_k_norm, attn_w_o, final_norm):
    b, s, d = x.shape
    depth = ffn1_norm.shape[0]
    n = b * s
    ffn1_f32 = (ffn1_w_gate, ffn1_w_up, ffn1_w_down)
    ffn2_f32 = (ffn2_w_gate, ffn2_w_up, ffn2_w_down)
    weights = tuple(w[0].astype(BF16) for w in ffn1_f32)

    def ffn(x3d, gain, weights, **kw):
        out, next_w = _ffn(x3d.reshape(n, d), gain.reshape(1, d), *weights, **kw)
        return out.reshape(b, s, d), next_w

    for i in range(depth):
        x, weights = ffn(x, ffn1_norm[i], weights, next_weights=(ffn2_f32, i))
        kind, j = i % N_MIXERS, i // N_MIXERS
        final_gain = final_norm if i == depth - 1 else None
        following = (ffn1_f32, i + 1) if i + 1 < depth else None
        if kind == 0:
            x, weights = _pool_ffn(x, mixer_norm[i], pool_w[j], pool_b[j], pool_scale[j],
                                   ffn2_norm[i].reshape(1, d), *weights,
                                   final_gain=final_gain, next_weights=following)
            continue
        proj = None
        if kind == 1:
            x = _fourier_mixer(x, mixer_norm[i], fourier_w[j], fourier_b[j])
        else:
            proj = _attention_mixer(x, mixer_norm[i], attn_w_qkv[j], attn_q_norm[j],
                                    attn_k_norm[j], attn_w_o[j])
        x, weights = ffn(x, ffn2_norm[i], weights, final_gain=final_gain, proj=proj,
                         next_weights=following)
    return x
```
